```python
import math
import jax
import jax.numpy as jnp
from jax import lax
import numpy as np

D_MODEL = 2048
BATCH = 2
SEQ = 4096
DEPTH = 4

N_MIXERS = 3
NORM_EPS = 1e-6
NEG_INF = -1e30

DN_QK_HEADS = 16
DN_V_HEADS = 32
DN_HEAD_DIM = 128
DN_CONV_WIDTH = 4
DN_CHUNK = 64
DN_QK_DIM = DN_QK_HEADS * DN_HEAD_DIM
DN_V_DIM = DN_V_HEADS * DN_HEAD_DIM
DN_CONV_CH = 2 * DN_QK_DIM + DN_V_DIM
DN_IN = DN_CONV_CH + DN_V_DIM + 2 * DN_V_HEADS

CV_INNER = D_MODEL
CV_WIDTH = 31

NSA_HEADS = 16
NSA_KV_GROUPS = 4
NSA_HEAD_DIM = 128
NSA_CMP_BLOCK = 32
NSA_CMP_STRIDE = 16
NSA_CMP_HIDDEN = 2 * NSA_HEAD_DIM
NSA_SEL_BLOCK = 64
NSA_TOP_N = 16
NSA_WINDOW = 512
NSA_Q_BLOCK = 128
NSA_SEL_Q_BLOCK = 64
NSA_FORCE = 1e9
NSA_Q_DIM = NSA_HEADS * NSA_HEAD_DIM
NSA_KV_DIM = NSA_KV_GROUPS * NSA_HEAD_DIM
NSA_IN = NSA_Q_DIM + 6 * NSA_KV_DIM + 3 * NSA_HEADS

FFN_DIM = 5504
N_EXPERTS = 8
TOP_K = 2
EXPERT_DIM = 7168

kernel_name = "hybrid_deltanet_conformer_nsa_moe"


def rms_norm(x, g):
    x32 = x.astype(jnp.float32)
    y = x32 * lax.rsqrt(jnp.mean(x32 * x32, axis=-1, keepdims=True) + NORM_EPS)
    return (y * g.astype(jnp.float32)).astype(x.dtype)


def causal_depthwise_conv(x, w):
    width, ch = w.shape
    return lax.conv_general_dilated(
        x, w[:, None, :], window_strides=(1,), padding=[(width - 1, 0)],
        dimension_numbers=("NWC", "WIO", "NWC"), feature_group_count=ch)


def l2_normalize(t):
    return t * lax.rsqrt(jnp.sum(t * t, axis=-1, keepdims=True) + 1e-6)


def chunked_gated_delta_rule(q, k, v, g, beta):
    bsz, nh, seqlen, dk = q.shape
    dv = v.shape[-1]
    c = DN_CHUNK
    n = seqlen // c
    q = (l2_normalize(q) * (dk ** -0.5)).reshape(bsz, nh, n, c, dk)
    k = l2_normalize(k).reshape(bsz, nh, n, c, dk)
    v = v.reshape(bsz, nh, n, c, dv)
    beta = beta.reshape(bsz, nh, n, c)
    g = jnp.cumsum(g.reshape(bsz, nh, n, c), axis=-1)
    causal = jnp.tril(jnp.ones((c, c), dtype=bool))
    decay = jnp.exp(jnp.where(causal, g[..., :, None] - g[..., None, :], -jnp.inf))
    kb = k * beta[..., None]
    strict = jnp.tril(jnp.ones((c, c), dtype=bool), -1)
    lower = jnp.where(strict, jnp.einsum("bhnid,bhnjd->bhnij", kb, k) * decay, 0.0) + jnp.eye(c, dtype=q.dtype)
    eye = jnp.broadcast_to(jnp.eye(c, dtype=q.dtype), lower.shape)
    tinv = lax.linalg.triangular_solve(lower, eye, left_side=True, lower=True, unit_diagonal=True)
    u = jnp.einsum("bhnij,bhnje->bhnie", tinv, v * beta[..., None])
    w = jnp.einsum("bhnij,bhnjd->bhnid", tinv, kb * jnp.exp(g)[..., None])
    attn = jnp.einsum("bhnid,bhnjd->bhnij", q, k) * decay
    q_dec = q * jnp.exp(g)[..., None]
    k_dec = k * jnp.exp(g[..., -1:] - g)[..., None]
    g_last = jnp.exp(g[..., -1])

    def step(state, xs):
        w_c, u_c, q_c, k_c, a_c, gl_c = xs
        v_new = u_c - jnp.einsum("bhcd,bhde->bhce", w_c, state)
        o = jnp.einsum("bhcd,bhde->bhce", q_c, state) + jnp.einsum("bhij,bhje->bhie", a_c, v_new)
        state = state * gl_c[..., None, None] + jnp.einsum("bhcd,bhce->bhde", k_c, v_new)
        return state, o

    xs = tuple(jnp.moveaxis(t, 2, 0) for t in (w, u, q_dec, k_dec, attn, g_last))
    state0 = jnp.zeros((bsz, nh, dk, dv), q.dtype)
    _, o = lax.scan(step, state0, xs)
    return jnp.moveaxis(o, 0, 2).reshape(bsz, nh, seqlen, dv)


def gated_deltanet(h, w_in, conv_w, a_log, dt_bias, o_norm, w_out):
    bsz, seqlen, _ = h.shape
    f32 = jnp.float32
    y = h @ w_in
    qkv, z, b, a = jnp.split(y, [DN_CONV_CH, DN_CONV_CH + DN_V_DIM, DN_CONV_CH + DN_V_DIM + DN_V_HEADS], axis=-1)
    qkv = jax.nn.silu(causal_depthwise_conv(qkv, conv_w))
    q, k, v = jnp.split(qkv, [DN_QK_DIM, 2 * DN_QK_DIM], axis=-1)
    rep = DN_V_HEADS // DN_QK_HEADS
    q = jnp.repeat(q.reshape(bsz, seqlen, DN_QK_HEADS, DN_HEAD_DIM), rep, axis=2)
    k = jnp.repeat(k.reshape(bsz, seqlen, DN_QK_HEADS, DN_HEAD_DIM), rep, axis=2)
    v = v.reshape(bsz, seqlen, DN_V_HEADS, DN_HEAD_DIM)
    beta = jax.nn.sigmoid(b.astype(f32))
    g = -jnp.exp(a_log.astype(f32)) * jax.nn.softplus(a.astype(f32) + dt_bias.astype(f32))
    to_bht = lambda t: jnp.swapaxes(t.astype(f32), 1, 2)
    o = chunked_gated_delta_rule(to_bht(q), to_bht(k), to_bht(v), to_bht(g), to_bht(beta))
    o = jnp.swapaxes(o, 1, 2)
    z = z.reshape(bsz, seqlen, DN_V_HEADS, DN_HEAD_DIM).astype(f32)
    o = o * lax.rsqrt(jnp.mean(o * o, axis=-1, keepdims=True) + NORM_EPS) * o_norm.astype(f32) * jax.nn.silu(z)
    return o.reshape(bsz, seqlen, DN_V_DIM).astype(h.dtype) @ w_out


def conformer_conv(h, w_in, b_in, dw, dw_b, ln_g, ln_b, w_out, b_out):
    u = h @ w_in + b_in
    val, gate = jnp.split(u, 2, axis=-1)
    u = val * jax.nn.sigmoid(gate)
    u = causal_depthwise_conv(u, dw) + dw_b
    u32 = u.astype(jnp.float32)
    mu = jnp.mean(u32, axis=-1, keepdims=True)
    var = jnp.mean(jnp.square(u32 - mu), axis=-1, keepdims=True)
    u32 = (u32 - mu) * lax.rsqrt(var + NORM_EPS) * ln_g + ln_b
    u = jax.nn.silu(u32).astype(h.dtype)
    return u @ w_out + b_out


def compress_blocks(t, pe, w1, w2):
    bsz, seqlen, ng, dh = t.shape
    ratio = NSA_CMP_BLOCK // NSA_CMP_STRIDE
    ns = seqlen // NSA_CMP_STRIDE
    nc = ns - ratio + 1
    s = t.reshape(bsz, ns, NSA_CMP_STRIDE, ng, dh)
    blk = jnp.concatenate([s[:, r:r + nc] for r in range(ratio)], axis=2)
    blk = blk + pe[None, None, :, None, :]
    blk = jnp.moveaxis(blk, 3, 2).reshape(bsz, nc, ng, NSA_CMP_BLOCK * dh)
    return jax.nn.silu(blk @ w1) @ w2


def nsa_compressed_branch(q, k_cmp, v_cmp, pos, cover):
    bsz, seqlen, ng, nr, dh = q.shape
    nc = k_cmp.shape[1]
    nq = seqlen // NSA_Q_BLOCK
    blk_end = jnp.arange(nc) * NSA_CMP_STRIDE + NSA_CMP_BLOCK - 1
    v32 = v_cmp.astype(jnp.float32)

    def attend(args):
        qc, pc = args
        valid = blk_end[None, :] <= pc[:, None]
        s = jnp.einsum("bqgrd,bngd->bgrqn", qc, k_cmp).astype(jnp.float32) * (dh ** -0.5)
        p = jax.nn.softmax(jnp.where(valid, s, NEG_INF), axis=-1)
        p = jnp.where(valid, p, 0.0)
        o = jnp.einsum("bgrqn,bngd->bqgrd", p, v32)
        imp = jnp.einsum("bgrqn,jn->bgqj", p, cover)
        return o, imp

    qb = jnp.moveaxis(q.reshape(bsz, nq, NSA_Q_BLOCK, ng, nr, dh), 1, 0)
    o, imp = lax.map(attend, (qb, pos.reshape(nq, NSA_Q_BLOCK)))
    o = jnp.moveaxis(o, 0, 1).reshape(bsz, seqlen, ng, nr, dh)
    imp = jnp.moveaxis(imp, 0, 2).reshape(bsz, ng, seqlen, -1)
    return o, imp


def nsa_selected_branch(q, k_sel, v_sel, imp, pos):
    bsz, seqlen, ng, nr, dh = q.shape
    n_sel = seqlen // NSA_SEL_BLOCK
    n_top = min(NSA_TOP_N, n_sel)
    blk = jnp.arange(n_sel)
    cur = pos // NSA_SEL_BLOCK
    forced = (blk[None, :] == 0) | (blk[None, :] == cur[:, None]) | (blk[None, :] == cur[:, None] - 1)
    causal = blk[None, :] * NSA_SEL_BLOCK <= pos[:, None]
    score = jnp.where(forced, NSA_FORCE, jnp.where(causal, imp, -NSA_FORCE))
    _, idx = lax.top_k(score, n_top)
    kb = jnp.moveaxis(k_sel.reshape(bsz, n_sel, NSA_SEL_BLOCK, ng, dh), 3, 1)
    vb = jnp.moveaxis(v_sel.reshape(bsz, n_sel, NSA_SEL_BLOCK, ng, dh), 3, 1)
    qblk = NSA_SEL_Q_BLOCK
    nq = seqlen // qblk
    q_blocks = jnp.moveaxis(q.reshape(bsz, nq, qblk, ng, nr, dh), 1, 0)
    idx_blocks = jnp.moveaxis(idx.reshape(bsz, ng, nq, qblk, n_top), 2, 0)
    pos_blocks = pos.reshape(nq, qblk)
    bi = jnp.arange(bsz)[:, None, None, None]
    gi = jnp.arange(ng)[None, :, None, None]
    offs = jnp.arange(NSA_SEL_BLOCK)

    def attend(args):
        qc, ic, pc = args
        kg = kb[bi, gi, ic]
        vg = vb[bi, gi, ic]
        s = jnp.einsum("bqgrd,bgqnkd->bgrqnk", qc, kg).astype(jnp.float32) * (dh ** -0.5)
        kpos = ic[..., None] * NSA_SEL_BLOCK + offs
        mask = (kpos <= pc[None, None, :, None, None])[:, :, None]
        s = jnp.where(mask, s, NEG_INF)
        p = jax.nn.softmax(s.reshape(bsz, ng, nr, qblk, n_top * NSA_SEL_BLOCK), axis=-1).reshape(s.shape)
        return jnp.einsum("bgrqnk,bgqnkd->bqgrd", p, vg.astype(jnp.float32))

    o = lax.map(attend, (q_blocks, idx_blocks, pos_blocks))
    return jnp.moveaxis(o, 0, 1).reshape(bsz, seqlen, ng, nr, dh)


def nsa_window_branch(q, k_win, v_win, pos):
    bsz, seqlen, ng, nr, dh = q.shape
    qblk = NSA_Q_BLOCK
    nb = seqlen // qblk
    nback = NSA_WINDOW // qblk
    pad = ((0, 0), (nback, 0), (0, 0), (0, 0), (0, 0))
    kp = jnp.pad(k_win.reshape(bsz, nb, qblk, ng, dh), pad)
    vp = jnp.pad(v_win.reshape(bsz, nb, qblk, ng, dh), pad)
    kband = jnp.concatenate([kp[:, s:s + nb] for s in range(nback + 1)], axis=2)
    vband = jnp.concatenate([vp[:, s:s + nb] for s in range(nback + 1)], axis=2)
    qb = q.reshape(bsz, nb, qblk, ng, nr, dh)
    qpos = pos.reshape(nb, qblk)
    kpos = (jnp.arange(nb)[:, None] - nback) * qblk + jnp.arange((nback + 1) * qblk)[None, :]
    diff = qpos[:, :, None] - kpos[:, None, :]
    mask = (kpos[:, None, :] >= 0) & (diff >= 0) & (diff < NSA_WINDOW)
    s = jnp.einsum("bnqgrd,bnkgd->bgrnqk", qb, kband).astype(jnp.float32) * (dh ** -0.5)
    p = jax.nn.softmax(jnp.where(mask, s, NEG_INF), axis=-1)
    o = jnp.einsum("bgrnqk,bnkgd->bnqgrd", p, vband.astype(jnp.float32))
    return o.reshape(bsz, seqlen, ng, nr, dh)


def nsa_attention(h, w_in, pe_k, pe_v, ck_w1, ck_w2, cv_w1, cv_w2, w_out):
    bsz, seqlen, _ = h.shape
    ng, nr, dh = NSA_KV_GROUPS, NSA_HEADS // NSA_KV_GROUPS, NSA_HEAD_DIM
    y = h @ w_in
    splits = np.cumsum([NSA_Q_DIM] + [NSA_KV_DIM] * 6).tolist()
    q, kc, vc, ks, vs, kw, vw, gl = jnp.split(y, splits, axis=-1)
    q = q.reshape(bsz, seqlen, ng, nr, dh)
    kc, vc, ks, vs, kw, vw = (t.reshape(bsz, seqlen, ng, dh) for t in (kc, vc, ks, vs, kw, vw))
    gates = jax.nn.sigmoid(gl.astype(jnp.float32)).reshape(bsz, seqlen, ng, nr, 3)
    pos = jnp.arange(seqlen)
    k_cmp = compress_blocks(kc, pe_k, ck_w1, ck_w2)
    v_cmp = compress_blocks(vc, pe_v, cv_w1, cv_w2)
    nc = k_cmp.shape[1]
    n_sel = seqlen // NSA_SEL_BLOCK
    sel_idx = jnp.arange(n_sel)[:, None, None]
    m_idx = jnp.arange(NSA_SEL_BLOCK // NSA_CMP_STRIDE)[None, :, None]
    n_idx = jnp.arange(NSA_CMP_BLOCK // NSA_CMP_STRIDE)[None, None, :]
    ci = (NSA_SEL_BLOCK // NSA_CMP_STRIDE) * sel_idx + m_idx - n_idx
    cover = jax.nn.one_hot(ci, nc, dtype=jnp.float32).sum(axis=(1, 2))
    o_cmp, imp = nsa_compressed_branch(q, k_cmp, v_cmp, pos, cover)
    o_sel = nsa_selected_branch(q, ks, vs, imp, pos)
    o_win = nsa_window_branch(q, kw, vw, pos)
    o = gates[..., 0:1] * o_cmp + gates[..., 1:2] * o_sel + gates[..., 2:3] * o_win
    return o.reshape(bsz, seqlen, NSA_Q_DIM).astype(h.dtype) @ w_out


def swiglu(h, w1, w3, w2):
    return (jax.nn.silu(h @ w1) * (h @ w3)) @ w2


def moe_swiglu(h, router, w1, w3, w2):
    bsz, seqlen, d = h.shape
    hf = h.reshape(bsz * seqlen, d)
    logits = (hf @ router).astype(jnp.float32)
    top_val, top_idx = lax.top_k(logits, TOP_K)
    top_w = jax.nn.softmax(top_val, axis=-1)
    gate = jnp.sum(jax.nn.one_hot(top_idx, N_EXPERTS, dtype=jnp.float32) * top_w[..., None], axis=1)
    out = jnp.zeros(hf.shape, jnp.float32)
    for e in range(N_EXPERTS):
        y = swiglu(hf, w1[e], w3[e], w2[e]).astype(jnp.float32)
        out = out + gate[:, e:e + 1] * y
    return out.astype(h.dtype).reshape(bsz, seqlen, d)


def setup_inputs(seed: int = 0) -> dict:
    key = jax.random.key(seed)
    keys = iter(jax.random.split(key, 40))
    f32 = jnp.float32
    n_a, n_b, n_c = [len(range(m, DEPTH, N_MIXERS)) for m in range(N_MIXERS)]
    n_dense = len(range(0, DEPTH, 2))
    n_moe = len(range(1, DEPTH, 2))

    def w(shape, fan_in):
        return jax.random.normal(next(keys), shape, f32) * (fan_in ** -0.5)

    def gain(shape):
        return 1.0 + 0.02 * jax.random.normal(next(keys), shape, f32)

    def small(shape, scale=0.02):
        return scale * jax.random.normal(next(keys), shape, f32)

    x = jax.random.normal(next(keys), (BATCH, SEQ, D_MODEL), f32)
    mix_norm = gain((DEPTH, D_MODEL))
    ffn_norm = gain((DEPTH, D_MODEL))
    final_norm = gain((D_MODEL,))
    dn_w_in = w((n_a, D_MODEL, DN_IN), D_MODEL)
    dn_conv = w((n_a, DN_CONV_WIDTH, DN_CONV_CH), DN_CONV_WIDTH)
    dn_a_log = jnp.log(jax.random.uniform(next(keys), (n_a, DN_V_HEADS), f32, 1.0, 16.0))
    dt = jnp.exp(jax.random.uniform(next(keys), (n_a, DN_V_HEADS), f32, math.log(1e-3), math.log(1e-1)))
    dn_dt_bias = dt + jnp.log(-jnp.expm1(-dt))
    dn_o_norm = gain((n_a, DN_HEAD_DIM))
    dn_w_out = w((n_a, DN_V_DIM, D_MODEL), DN_V_DIM)
    cv_w_in = w((n_b, D_MODEL, 2 * CV_INNER), D_MODEL)
    cv_b_in = small((n_b, 2 * CV_INNER))
    cv_dw = w((n_b, CV_WIDTH, CV_INNER), CV_WIDTH)
    cv_dw_b = small((n_b, CV_INNER))
    cv_ln_g = gain((n_b, CV_INNER))
    cv_ln_b = small((n_b, CV_INNER))
    cv_w_out = w((n_b, CV_INNER, D_MODEL), CV_INNER)
    cv_b_out = small((n_b, D_MODEL))
    nsa_w_in = w((n_c, D_MODEL, NSA_IN), D_MODEL)
    nsa_pe_k = small((n_c, NSA_CMP_BLOCK, NSA_HEAD_DIM), 0.1)
    nsa_pe_v = small((n_c, NSA_CMP_BLOCK, NSA_HEAD_DIM), 0.1)
    nsa_ck_w1 = w((n_c, NSA_CMP_BLOCK * NSA_HEAD_DIM, NSA_CMP_HIDDEN), NSA_CMP_BLOCK * NSA_HEAD_DIM)
    nsa_ck_w2 = w((n_c, NSA_CMP_HIDDEN, NSA_HEAD_DIM), NSA_CMP_HIDDEN)
    nsa_cv_w1 = w((n_c, NSA_CMP_BLOCK * NSA_HEAD_DIM, NSA_CMP_HIDDEN), NSA_CMP_BLOCK * NSA_HEAD_DIM)
    nsa_cv_w2 = w((n_c, NSA_CMP_HIDDEN, NSA_HEAD_DIM), NSA_CMP_HIDDEN)
    nsa_w_out = w((n_c, NSA_Q_DIM, D_MODEL), NSA_Q_DIM)
    ffn_w1 = w((n_dense, D_MODEL, FFN_DIM), D_MODEL)
    ffn_w3 = w((n_dense, D_MODEL, FFN_DIM), D_MODEL)
    ffn_w2 = w((n_dense, FFN_DIM, D_MODEL), FFN_DIM)
    moe_router = w((n_moe, D_MODEL, N_EXPERTS), D_MODEL)
    moe_w1 = w((n_moe, N_EXPERTS, D_MODEL, EXPERT_DIM), D_MODEL)
    moe_w3 = w((n_moe, N_EXPERTS, D_MODEL, EXPERT_DIM), D_MODEL)
    moe_w2 = w((n_moe, N_EXPERTS, EXPERT_DIM, D_MODEL), EXPERT_DIM)
    return {
        "x": x, "mix_norm": mix_norm, "ffn_norm": ffn_norm, "final_norm": final_norm,
        "dn_w_in": dn_w_in, "dn_conv": dn_conv, "dn_a_log": dn_a_log, "dn_dt_bias": dn_dt_bias,
        "dn_o_norm": dn_o_norm, "dn_w_out": dn_w_out,
        "cv_w_in": cv_w_in, "cv_b_in": cv_b_in, "cv_dw": cv_dw, "cv_dw_b": cv_dw_b,
        "cv_ln_g": cv_ln_g, "cv_ln_b": cv_ln_b, "cv_w_out": cv_w_out, "cv_b_out": cv_b_out,
        "nsa_w_in": nsa_w_in, "nsa_pe_k": nsa_pe_k, "nsa_pe_v": nsa_pe_v,
        "nsa_ck_w1": nsa_ck_w1, "nsa_ck_w2": nsa_ck_w2, "nsa_cv_w1": nsa_cv_w1, "nsa_cv_w2": nsa_cv_w2,
        "nsa_w_out": nsa_w_out,
        "ffn_w1": ffn_w1, "ffn_w3": ffn_w3, "ffn_w2": ffn_w2,
        "moe_router": moe_router, "moe_w1": moe_w1, "moe_w3": moe_w3, "moe_w2": moe_w2,
    }


def reference(x, mix_norm, ffn_norm, final_norm,
              dn_w_in, dn_conv, dn_a_log, dn_dt_bias, dn_o_norm, dn_w_out,
              cv_w_in, cv_b_in, cv_dw, cv_dw_b, cv_ln_g, cv_ln_b, cv_w_out, cv_b_out,
              nsa_w_in, nsa_pe_k, nsa_pe_v, nsa_ck_w1, nsa_ck_w2, nsa_cv_w1, nsa_cv_w2, nsa_w_out,
              ffn_w1, ffn_w3, ffn_w2,
              moe_router, moe_w1, moe_w3, moe_w2):
    h = x
    mixer_count = [0, 0, 0]
    for i in range(DEPTH):
        u = rms_norm(h, mix_norm[i])
        kind = i % N_MIXERS
        j = mixer_count[kind]
        mixer_count[kind] += 1
        if kind == 0:
            m = gated_deltanet(u, dn_w_in[j], dn_conv[j], dn_a_log[j], dn_dt_bias[j], dn_o_norm[j], dn_w_out[j])
        elif kind == 1:
            m = conformer_conv(u, cv_w_in[j], cv_b_in[j], cv_dw[j], cv_dw_b[j], cv_ln_g[j], cv_ln_b[j],
                               cv_w_out[j], cv_b_out[j])
        else:
            m = nsa_attention(u, nsa_w_in[j], nsa_pe_k[j], nsa_pe_v[j], nsa_ck_w1[j], nsa_ck_w2[j],
                              nsa_cv_w1[j], nsa_cv_w2[j], nsa_w_out[j])
        h = h + m
        u = rms_norm(h, ffn_norm[i])
        if i % 2 == 0:
            f = swiglu(u, ffn_w1[i // 2], ffn_w3[i // 2], ffn_w2[i // 2])
        else:
            f = moe_swiglu(u, moe_router[i // 2], moe_w1[i // 2], moe_w3[i // 2], moe_w2[i // 2])
        h = h + f
    return rms_norm(h, final_norm)
```

```python
import functools
import math

import numpy as np
import jax
import jax.numpy as jnp
from jax import lax
from jax.experimental import pallas as pl
from jax.experimental.pallas import tpu as pltpu

F32 = jnp.float32
BF16 = jnp.bfloat16

V7X_VMEM_LIMIT_BYTES = 56 * 1024 * 1024

NORM_EPS = 1e-6
NEG_INF = -1e30

DN_QK_HEADS = 16
DN_V_HEADS = 32
DN_HEAD_DIM = 128
DN_CHUNK = 64

NSA_HEADS = 16
NSA_KV_GROUPS = 4
NSA_HEAD_DIM = 128
NSA_CMP_BLOCK = 32
NSA_CMP_STRIDE = 16
NSA_SEL_BLOCK = 64
NSA_TOP_N = 16
NSA_WINDOW = 512
NSA_Q_BLOCK = 128
NSA_SEL_Q_BLOCK = 64
NSA_FORCE = 1e9

N_EXPERTS = 8
TOP_K = 2
MOE_ROW_TILE = 512


def _cparams(*sem):
    return pltpu.CompilerParams(dimension_semantics=sem, vmem_limit_bytes=V7X_VMEM_LIMIT_BYTES)


def _weight_changed(te_ref, i):
    prev = te_ref[jnp.maximum(i - 1, 0)]
    return jnp.logical_or(i == 0, te_ref[i] != prev)


def _gmm_body(te_ref, nu_ref, x_ref, w_ref, *rest, has_bias, has_res):
    rest = list(rest)
    b_ref = rest.pop(0) if has_bias else None
    r_ref = rest.pop(0) if has_res else None
    o_ref, wb_ref = rest
    i = pl.program_id(1)

    @pl.when(_weight_changed(te_ref, i))
    def _():
        wb_ref[...] = w_ref[...].astype(BF16)

    @pl.when(i < nu_ref[0])
    def _():
        acc = jnp.dot(x_ref[...], wb_ref[...], preferred_element_type=F32)
        if has_bias:
            acc = acc + b_ref[...]
        if has_res:
            acc = acc + r_ref[...]
        o_ref[...] = acc.astype(o_ref.dtype)

    @pl.when(i >= nu_ref[0])
    def _():
        o_ref[...] = jnp.zeros(o_ref.shape, o_ref.dtype)


def _gmm(x, w, te, nu, *, bias=None, res=None, out_dtype=F32, tm, tn):
    m, k = x.shape
    n = w.shape[2]
    assert m % tm == 0 and w.shape[1] == k
    grid = (pl.cdiv(n, tn), m // tm)
    in_specs = [
        pl.BlockSpec((tm, k), lambda j, i, te, nu: (i, 0)),
        pl.BlockSpec((None, k, tn), lambda j, i, te, nu: (te[i], 0, j)),
    ]
    args = [x, w]
    if bias is not None:
        in_specs.append(pl.BlockSpec((1, tn), lambda j, i, te, nu: (0, j)))
        args.append(bias.reshape(1, n).astype(F32))
    if res is not None:
        in_specs.append(pl.BlockSpec((tm, tn), lambda j, i, te, nu: (i, j)))
        args.append(res)
    return pl.pallas_call(
        functools.partial(_gmm_body, has_bias=bias is not None, has_res=res is not None),
        grid_spec=pltpu.PrefetchScalarGridSpec(
            num_scalar_prefetch=2,
            grid=grid,
            in_specs=in_specs,
            out_specs=pl.BlockSpec((tm, tn), lambda j, i, te, nu: (i, j)),
            scratch_shapes=[pltpu.VMEM((k, tn), BF16)],
        ),
        out_shape=jax.ShapeDtypeStruct((m, n), out_dtype),
        compiler_params=_cparams("parallel", "arbitrary"),
    )(te, nu, *args)


def _gglu_body(te_ref, nu_ref, x_ref, w1_ref, w3_ref, o_ref, w1b_ref, w3b_ref):
    i = pl.program_id(1)

    @pl.when(_weight_changed(te_ref, i))
    def _():
        w1b_ref[...] = w1_ref[...].astype(BF16)
        w3b_ref[...] = w3_ref[...].astype(BF16)

    @pl.when(i < nu_ref[0])
    def _():
        x = x_ref[...]
        a = jnp.dot(x, w1b_ref[...], preferred_element_type=F32)
        b = jnp.dot(x, w3b_ref[...], preferred_element_type=F32)
        o_ref[...] = (a * jax.nn.sigmoid(a) * b).astype(o_ref.dtype)

    @pl.when(i >= nu_ref[0])
    def _():
        o_ref[...] = jnp.zeros(o_ref.shape, o_ref.dtype)


def _gglu(x, w1, w3, te, nu, *, tm, tn):
    m, k = x.shape
    n = w1.shape[2]
    assert m % tm == 0
    grid = (pl.cdiv(n, tn), m // tm)
    wspec = pl.BlockSpec((None, k, tn), lambda j, i, te, nu: (te[i], 0, j))
    return pl.pallas_call(
        _gglu_body,
        grid_spec=pltpu.PrefetchScalarGridSpec(
            num_scalar_prefetch=2,
            grid=grid,
            in_specs=[pl.BlockSpec((tm, k), lambda j, i, te, nu: (i, 0)), wspec, wspec],
            out_specs=pl.BlockSpec((tm, tn), lambda j, i, te, nu: (i, j)),
            scratch_shapes=[pltpu.VMEM((k, tn), BF16), pltpu.VMEM((k, tn), BF16)],
        ),
        out_shape=jax.ShapeDtypeStruct((m, n), BF16),
        compiler_params=_cparams("parallel", "arbitrary"),
    )(te, nu, x, w1, w3)


def _dense_tiles(m, tm, slab):
    nt = m // tm
    return jnp.full((nt,), slab, jnp.int32), jnp.full((1,), nt, jnp.int32)


def _mm(x, w, slab, *, bias=None, res=None, out_dtype=F32, tm=512, tn=512):
    te, nu = _dense_tiles(x.shape[0], tm, slab)
    return _gmm(x.astype(BF16), w, te, nu, bias=bias, res=res, out_dtype=out_dtype, tm=tm, tn=tn)


def _rms_norm(x, g):
    return x * lax.rsqrt(jnp.mean(x * x, axis=-1, keepdims=True) + NORM_EPS) * g


def _swiglu_res(h, u, w1, w3, w2, layer):
    ub = u.astype(BF16)
    te, nu = _dense_tiles(ub.shape[0], 512, layer)
    act = _gglu(ub, w1, w3, te, nu, tm=512, tn=512)
    return _gmm(act, w2, te, nu, res=h, tm=512, tn=512)


def _moe_res(h, u, router, w1, w3, w2, layer):
    m, d = u.shape
    ts = MOE_ROW_TILE
    w1, w3, w2 = (w.reshape((-1,) + w.shape[2:]) for w in (w1, w3, w2))
    logits = jnp.dot(u, router, precision=lax.Precision.HIGHEST)
    top_val, top_idx = lax.top_k(logits, TOP_K)
    top_w = jax.nn.softmax(top_val, axis=-1)
    flat_e = top_idx.reshape(-1).astype(jnp.int32)
    order = jnp.argsort(flat_e, stable=True).astype(jnp.int32)
    sizes = jnp.zeros((N_EXPERTS,), jnp.int32).at[flat_e].add(1)
    starts = jnp.cumsum(sizes) - sizes
    psizes = ((sizes + ts - 1) // ts) * ts
    pends = jnp.cumsum(psizes)
    pstarts = pends - psizes
    sorted_e = flat_e[order]
    dest = pstarts[sorted_e] + (jnp.arange(m * TOP_K, dtype=jnp.int32) - starts[sorted_e])
    mp = m * TOP_K + N_EXPERTS * ts
    row_token = jnp.zeros((mp,), jnp.int32).at[dest].set(order // TOP_K)
    pos = jnp.zeros((m * TOP_K,), jnp.int32).at[order].set(dest)
    nt = mp // ts
    te = jnp.minimum(jnp.searchsorted(pends, jnp.arange(nt, dtype=jnp.int32) * ts, side="right"),
                     N_EXPERTS - 1).astype(jnp.int32) + layer * N_EXPERTS
    nu = (pends[-1] // ts).astype(jnp.int32).reshape(1)
    xs = jnp.take(u.astype(BF16), row_token, axis=0)
    act = _gglu(xs, w1, w3, te, nu, tm=ts, tn=512)
    ys = _gmm(act, w2, te, nu, tm=ts, tn=256)
    yk = jnp.take(ys, pos, axis=0).reshape(m, TOP_K, d)
    return h + jnp.sum(yk * top_w[..., None], axis=1)


def _causal_depthwise_conv(x, w):
    width, ch = w.shape
    return lax.conv_general_dilated(
        x, w[:, None, :], window_strides=(1,), padding=[(width - 1, 0)],
        dimension_numbers=("NWC", "WIO", "NWC"), feature_group_count=ch)


def _l2_normalize(t):
    return t * lax.rsqrt(jnp.sum(t * t, axis=-1, keepdims=True) + 1e-6)


def _chunked_gated_delta_rule(q, k, v, g, beta):
    bsz, nh, seqlen, dk = q.shape
    dv = v.shape[-1]
    c = DN_CHUNK
    n = seqlen // c
    q = (_l2_normalize(q) * (dk ** -0.5)).reshape(bsz, nh, n, c, dk)
    k = _l2_normalize(k).reshape(bsz, nh, n, c, dk)
    v = v.reshape(bsz, nh, n, c, dv)
    beta = beta.reshape(bsz, nh, n, c)
    g = jnp.cumsum(g.reshape(bsz, nh, n, c), axis=-1)
    causal = jnp.tril(jnp.ones((c, c), dtype=bool))
    decay = jnp.exp(jnp.where(causal, g[..., :, None] - g[..., None, :], -jnp.inf))
    kb = k * beta[..., None]
    strict = jnp.tril(jnp.ones((c, c), dtype=bool), -1)
    lower = jnp.where(strict, jnp.einsum("bhnid,bhnjd->bhnij", kb, k) * decay, 0.0) + jnp.eye(c, dtype=q.dtype)
    eye = jnp.broadcast_to(jnp.eye(c, dtype=q.dtype), lower.shape)
    tinv = lax.linalg.triangular_solve(lower, eye, left_side=True, lower=True, unit_diagonal=True)
    u = jnp.einsum("bhnij,bhnje->bhnie", tinv, v * beta[..., None])
    w = jnp.einsum("bhnij,bhnjd->bhnid", tinv, kb * jnp.exp(g)[..., None])
    attn = jnp.einsum("bhnid,bhnjd->bhnij", q, k) * decay
    q_dec = q * jnp.exp(g)[..., None]
    k_dec = k * jnp.exp(g[..., -1:] - g)[..., None]
    g_last = jnp.exp(g[..., -1])

    def step(state, xs):
        w_c, u_c, q_c, k_c, a_c, gl_c = xs
        v_new = u_c - jnp.einsum("bhcd,bhde->bhce", w_c, state)
        o = jnp.einsum("bhcd,bhde->bhce", q_c, state) + jnp.einsum("bhij,bhje->bhie", a_c, v_new)
        state = state * gl_c[..., None, None] + jnp.einsum("bhcd,bhce->bhde", k_c, v_new)
        return state, o

    xs = tuple(jnp.moveaxis(t, 2, 0) for t in (w, u, q_dec, k_dec, attn, g_last))
    state0 = jnp.zeros((bsz, nh, dk, dv), q.dtype)
    _, o = lax.scan(step, state0, xs)
    return jnp.moveaxis(o, 0, 2).reshape(bsz, nh, seqlen, dv)


def _gated_deltanet_res(h, u, bsz, seqlen, j, w_in, conv_w, a_log, dt_bias, o_norm, w_out):
    qk_dim = DN_QK_HEADS * DN_HEAD_DIM
    v_dim = DN_V_HEADS * DN_HEAD_DIM
    conv_ch = 2 * qk_dim + v_dim
    y = _mm(u, w_in, j, tm=1024, tn=512).reshape(bsz, seqlen, -1)
    qkv, z, b, a = jnp.split(y, [conv_ch, conv_ch + v_dim, conv_ch + v_dim + DN_V_HEADS], axis=-1)
    qkv = jax.nn.silu(_causal_depthwise_conv(qkv, conv_w))
    q, k, v = jnp.split(qkv, [qk_dim, 2 * qk_dim], axis=-1)
    rep = DN_V_HEADS // DN_QK_HEADS
    q = jnp.repeat(q.reshape(bsz, seqlen, DN_QK_HEADS, DN_HEAD_DIM), rep, axis=2)
    k = jnp.repeat(k.reshape(bsz, seqlen, DN_QK_HEADS, DN_HEAD_DIM), rep, axis=2)
    v = v.reshape(bsz, seqlen, DN_V_HEADS, DN_HEAD_DIM)
    beta = jax.nn.sigmoid(b)
    g = -jnp.exp(a_log) * jax.nn.softplus(a + dt_bias)
    to_bht = lambda t: jnp.swapaxes(t, 1, 2)
    o = _chunked_gated_delta_rule(to_bht(q), to_bht(k), to_bht(v), to_bht(g), to_bht(beta))
    o = jnp.swapaxes(o, 1, 2)
    z = z.reshape(bsz, seqlen, DN_V_HEADS, DN_HEAD_DIM)
    o = o * lax.rsqrt(jnp.mean(o * o, axis=-1, keepdims=True) + NORM_EPS) * o_norm * jax.nn.silu(z)
    return _mm(o.reshape(bsz * seqlen, v_dim), w_out, j, res=h, tm=512, tn=512)


def _conformer_res(h, u, bsz, seqlen, j, w_in, b_in, dw, dw_b, ln_g, ln_b, w_out, b_out):
    y = _mm(u, w_in, j, bias=b_in, tm=1024, tn=512).reshape(bsz, seqlen, -1)
    val, gate = jnp.split(y, 2, axis=-1)
    y = val * jax.nn.sigmoid(gate)
    y = _causal_depthwise_conv(y, dw) + dw_b
    mu = jnp.mean(y, axis=-1, keepdims=True)
    var = jnp.mean(jnp.square(y - mu), axis=-1, keepdims=True)
    y = (y - mu) * lax.rsqrt(var + NORM_EPS) * ln_g + ln_b
    y = jax.nn.silu(y)
    return _mm(y.reshape(bsz * seqlen, -1), w_out, j, bias=b_out, res=h, tm=512, tn=512)


def _compress_blocks(t, pe, w1, w2):
    bsz, seqlen, ng, dh = t.shape
    ratio = NSA_CMP_BLOCK // NSA_CMP_STRIDE
    ns = seqlen // NSA_CMP_STRIDE
    nc = ns - ratio + 1
    s = t.reshape(bsz, ns, NSA_CMP_STRIDE, ng, dh)
    blk = jnp.concatenate([s[:, r:r + nc] for r in range(ratio)], axis=2)
    blk = blk + pe[None, None, :, None, :]
    blk = jnp.moveaxis(blk, 3, 2).reshape(bsz, nc, ng, NSA_CMP_BLOCK * dh)
    return jax.nn.silu(blk @ w1) @ w2


def _nsa_compressed_branch(q, k_cmp, v_cmp, pos, cover):
    bsz, seqlen, ng, nr, dh = q.shape
    nc = k_cmp.shape[1]
    nq = seqlen // NSA_Q_BLOCK
    blk_end = jnp.arange(nc) * NSA_CMP_STRIDE + NSA_CMP_BLOCK - 1

    def attend(args):
        qc, pc = args
        valid = blk_end[None, :] <= pc[:, None]
        s = jnp.einsum("bqgrd,bngd->bgrqn", qc, k_cmp) * (dh ** -0.5)
        p = jax.nn.softmax(jnp.where(valid, s, NEG_INF), axis=-1)
        p = jnp.where(valid, p, 0.0)
        o = jnp.einsum("bgrqn,bngd->bqgrd", p, v_cmp)
        imp = jnp.einsum("bgrqn,jn->bgqj", p, cover)
        return o, imp

    qb = jnp.moveaxis(q.reshape(bsz, nq, NSA_Q_BLOCK, ng, nr, dh), 1, 0)
    o, imp = lax.map(attend, (qb, pos.reshape(nq, NSA_Q_BLOCK)))
    o = jnp.moveaxis(o, 0, 1).reshape(bsz, seqlen, ng, nr, dh)
    imp = jnp.moveaxis(imp, 0, 2).reshape(bsz, ng, seqlen, -1)
    return o, imp


def _nsa_selected_branch(q, k_sel, v_sel, imp, pos):
    bsz, seqlen, ng, nr, dh = q.shape
    n_sel = seqlen // NSA_SEL_BLOCK
    n_top = min(NSA_TOP_N, n_sel)
    blk = jnp.arange(n_sel)
    cur = pos // NSA_SEL_BLOCK
    forced = (blk[None, :] == 0) | (blk[None, :] == cur[:, None]) | (blk[None, :] == cur[:, None] - 1)
    causal = blk[None, :] * NSA_SEL_BLOCK <= pos[:, None]
    score = jnp.where(forced, NSA_FORCE, jnp.where(causal, imp, -NSA_FORCE))
    _, idx = lax.top_k(score, n_top)
    kb = jnp.moveaxis(k_sel.reshape(bsz, n_sel, NSA_SEL_BLOCK, ng, dh), 3, 1)
    vb = jnp.moveaxis(v_sel.reshape(bsz, n_sel, NSA_SEL_BLOCK, ng, dh), 3, 1)
    qblk = NSA_SEL_Q_BLOCK
    nq = seqlen // qblk
    q_blocks = jnp.moveaxis(q.reshape(bsz, nq, qblk, ng, nr, dh), 1, 0)
    idx_blocks = jnp.moveaxis(idx.reshape(bsz, ng, nq, qblk, n_top), 2, 0)
    pos_blocks = pos.reshape(nq, qblk)
    bi = jnp.arange(bsz)[:, None, None, None]
    gi = jnp.arange(ng)[None, :, None, None]
    offs = jnp.arange(NSA_SEL_BLOCK)

    def attend(args):
        qc, ic, pc = args
        kg = kb[bi, gi, ic]
        vg = vb[bi, gi, ic]
        s = jnp.einsum("bqgrd,bgqnkd->bgrqnk", qc, kg) * (dh ** -0.5)
        kpos = ic[..., None] * NSA_SEL_BLOCK + offs
        mask = (kpos <= pc[None, None, :, None, None])[:, :, None]
        s = jnp.where(mask, s, NEG_INF)
        p = jax.nn.softmax(s.reshape(bsz, ng, nr, qblk, n_top * NSA_SEL_BLOCK), axis=-1).reshape(s.shape)
        return jnp.einsum("bgrqnk,bgqnkd->bqgrd", p, vg)

    o = lax.map(attend, (q_blocks, idx_blocks, pos_blocks))
    return jnp.moveaxis(o, 0, 1).reshape(bsz, seqlen, ng, nr, dh)


def _nsa_window_branch(q, k_win, v_win, pos):
    bsz, seqlen, ng, nr, dh = q.shape
    qblk = NSA_Q_BLOCK
    nb = seqlen // qblk
    nback = NSA_WINDOW // qblk
    pad = ((0, 0), (nback, 0), (0, 0), (0, 0), (0, 0))
    kp = jnp.pad(k_win.reshape(bsz, nb, qblk, ng, dh), pad)
    vp = jnp.pad(v_win.reshape(bsz, nb, qblk, ng, dh), pad)
    kband = jnp.concatenate([kp[:, s:s + nb] for s in range(nback + 1)], axis=2)
    vband = jnp.concatenate([vp[:, s:s + nb] for s in range(nback + 1)], axis=2)
    qb = q.reshape(bsz, nb, qblk, ng, nr, dh)
    qpos = pos.reshape(nb, qblk)
    kpos = (jnp.arange(nb)[:, None] - nback) * qblk + jnp.arange((nback + 1) * qblk)[None, :]
    diff = qpos[:, :, None] - kpos[:, None, :]
    mask = (kpos[:, None, :] >= 0) & (diff >= 0) & (diff < NSA_WINDOW)
    s = jnp.einsum("bnqgrd,bnkgd->bgrnqk", qb, kband) * (dh ** -0.5)
    p = jax.nn.softmax(jnp.where(mask, s, NEG_INF), axis=-1)
    o = jnp.einsum("bgrnqk,bnkgd->bnqgrd", p, vband)
    return o.reshape(bsz, seqlen, ng, nr, dh)


def _nsa_res(h, u, bsz, seqlen, j, w_in, pe_k, pe_v, ck_w1, ck_w2, cv_w1, cv_w2, w_out):
    ng, nr, dh = NSA_KV_GROUPS, NSA_HEADS // NSA_KV_GROUPS, NSA_HEAD_DIM
    q_dim, kv_dim = NSA_HEADS * dh, ng * dh
    y = _mm(u, w_in, j, tm=1024, tn=256).reshape(bsz, seqlen, -1)
    splits = np.cumsum([q_dim] + [kv_dim] * 6).tolist()
    q, kc, vc, ks, vs, kw, vw, gl = jnp.split(y, splits, axis=-1)
    q = q.reshape(bsz, seqlen, ng, nr, dh)
    kc, vc, ks, vs, kw, vw = (t.reshape(bsz, seqlen, ng, dh) for t in (kc, vc, ks, vs, kw, vw))
    gates = jax.nn.sigmoid(gl).reshape(bsz, seqlen, ng, nr, 3)
    pos = jnp.arange(seqlen)
    k_cmp = _compress_blocks(kc, pe_k, ck_w1, ck_w2)
    v_cmp = _compress_blocks(vc, pe_v, cv_w1, cv_w2)
    nc = k_cmp.shape[1]
    n_sel = seqlen // NSA_SEL_BLOCK
    sel_idx = jnp.arange(n_sel)[:, None, None]
    m_idx = jnp.arange(NSA_SEL_BLOCK // NSA_CMP_STRIDE)[None, :, None]
    n_idx = jnp.arange(NSA_CMP_BLOCK // NSA_CMP_STRIDE)[None, None, :]
    ci = (NSA_SEL_BLOCK // NSA_CMP_STRIDE) * sel_idx + m_idx - n_idx
    cover = jax.nn.one_hot(ci, nc, dtype=F32).sum(axis=(1, 2))
    o_cmp, imp = _nsa_compressed_branch(q, k_cmp, v_cmp, pos, cover)
    o_sel = _nsa_selected_branch(q, ks, vs, imp, pos)
    o_win = _nsa_window_branch(q, kw, vw, pos)
    o = gates[..., 0:1] * o_cmp + gates[..., 1:2] * o_sel + gates[..., 2:3] * o_win
    return _mm(o.reshape(bsz * seqlen, q_dim), w_out, j, res=h, tm=512, tn=512)


def kernel(x, mix_norm, ffn_norm, final_norm, dn_w_in, dn_conv, dn_a_log, dn_dt_bias, dn_o_norm, dn_w_out, cv_w_in, cv_b_in, cv_dw, cv_dw_b, cv_ln_g, cv_ln_b, cv_w_out, cv_b_out, nsa_w_in, nsa_pe_k, nsa_pe_v, nsa_ck_w1, nsa_ck_w2, nsa_cv_w1, nsa_cv_w2, nsa_w_out, ffn_w1, ffn_w3, ffn_w2, moe_router, moe_w1, moe_w3, moe_w2):
    bsz, seqlen, d = x.shape
    depth = mix_norm.shape[0]
    h = x.reshape(bsz * seqlen, d)
    counts = [0, 0, 0]
    for i in range(depth):
        u = _rms_norm(h, mix_norm[i])
        kind = i % 3
        j = counts[kind]
        counts[kind] += 1
        if kind == 0:
            h = _gated_deltanet_res(h, u, bsz, seqlen, j, dn_w_in, dn_conv[j], dn_a_log[j], dn_dt_bias[j],
                                    dn_o_norm[j], dn_w_out)
        elif kind == 1:
            h = _conformer_res(h, u, bsz, seqlen, j, cv_w_in, cv_b_in[j], cv_dw[j], cv_dw_b[j], cv_ln_g[j],
                               cv_ln_b[j], cv_w_out, cv_b_out[j])
        else:
            h = _nsa_res(h, u, bsz, seqlen, j, nsa_w_in, nsa_pe_k[j], nsa_pe_v[j], nsa_ck_w1[j], nsa_ck_w2[j],
                         nsa_cv_w1[j], nsa_cv_w2[j], nsa_w_out)
        u = _rms_norm(h, ffn_norm[i])
        if i % 2 == 0:
            h = _swiglu_res(h, u, ffn_w1, ffn_w3, ffn_w2, i // 2)
        else:
            h = _moe_res(h, u, moe_router[i // 2], moe_w1, moe_w3, moe_w2, i // 2)
    return _rms_norm(h, final_norm).reshape(bsz, seqlen, d)
```

```python
import functools

import numpy as np
import jax
import jax.numpy as jnp
from jax import lax
from jax.experimental import pallas as pl
from jax.experimental.pallas import tpu as pltpu

F32 = jnp.float32
BF16 = jnp.bfloat16

V7X_VMEM_LIMIT_BYTES = 56 * 1024 * 1024

NORM_EPS = 1e-6
NEG_INF = -1e30

DN_QK_HEADS = 16
DN_V_HEADS = 32
DN_HEAD_DIM = 128
DN_CHUNK = 64

NSA_HEADS = 16
NSA_KV_GROUPS = 4
NSA_HEAD_DIM = 128
NSA_CMP_BLOCK = 32
NSA_CMP_STRIDE = 16
NSA_SEL_BLOCK = 64
NSA_TOP_N = 16
NSA_WINDOW = 512
NSA_FORCE = 1e9

N_EXPERTS = 8
TOP_K = 2
MOE_ROW_TILE = 512

_NT = (((1,), (1,)), ((), ()))
_TN = (((0,), (0,)), ((), ()))


def _cparams(*sem):
    return pltpu.CompilerParams(dimension_semantics=sem, vmem_limit_bytes=V7X_VMEM_LIMIT_BYTES)


def _weight_changed(te_ref, i):
    prev = te_ref[jnp.maximum(i - 1, 0)]
    return jnp.logical_or(i == 0, te_ref[i] != prev)


def _gmm_body(te_ref, nu_ref, x_ref, w_ref, *rest, has_bias, has_res):
    rest = list(rest)
    b_ref = rest.pop(0) if has_bias else None
    r_ref = rest.pop(0) if has_res else None
    o_ref, wb_ref = rest
    i = pl.program_id(1)

    @pl.when(_weight_changed(te_ref, i))
    def _():
        wb_ref[...] = w_ref[...].astype(BF16)

    @pl.when(i < nu_ref[0])
    def _():
        acc = jnp.dot(x_ref[...], wb_ref[...], preferred_element_type=F32)
        if has_bias:
            acc = acc + b_ref[...]
        if has_res:
            acc = acc + r_ref[...]
        o_ref[...] = acc.astype(o_ref.dtype)

    @pl.when(i >= nu_ref[0])
    def _():
        o_ref[...] = jnp.zeros(o_ref.shape, o_ref.dtype)


def _gmm(x, w, te, nu, *, bias=None, res=None, out_dtype=F32, tm, tn):
    m, k = x.shape
    n = w.shape[2]
    assert m % tm == 0 and w.shape[1] == k
    grid = (pl.cdiv(n, tn), m // tm)
    in_specs = [
        pl.BlockSpec((tm, k), lambda j, i, te, nu: (i, 0)),
        pl.BlockSpec((None, k, tn), lambda j, i, te, nu: (te[i], 0, j)),
    ]
    args = [x, w]
    if bias is not None:
        in_specs.append(pl.BlockSpec((1, tn), lambda j, i, te, nu: (0, j)))
        args.append(bias.reshape(1, n).astype(F32))
    if res is not None:
        in_specs.append(pl.BlockSpec((tm, tn), lambda j, i, te, nu: (i, j)))
        args.append(res)
    return pl.pallas_call(
        functools.partial(_gmm_body, has_bias=bias is not None, has_res=res is not None),
        grid_spec=pltpu.PrefetchScalarGridSpec(
            num_scalar_prefetch=2,
            grid=grid,
            in_specs=in_specs,
            out_specs=pl.BlockSpec((tm, tn), lambda j, i, te, nu: (i, j)),
            scratch_shapes=[pltpu.VMEM((k, tn), BF16)],
        ),
        out_shape=jax.ShapeDtypeStruct((m, n), out_dtype),
        compiler_params=_cparams("parallel", "arbitrary"),
    )(te, nu, *args)


def _gglu_body(te_ref, nu_ref, x_ref, w1_ref, w3_ref, o_ref, w1b_ref, w3b_ref):
    i = pl.program_id(1)

    @pl.when(_weight_changed(te_ref, i))
    def _():
        w1b_ref[...] = w1_ref[...].astype(BF16)
        w3b_ref[...] = w3_ref[...].astype(BF16)

    @pl.when(i < nu_ref[0])
    def _():
        x = x_ref[...]
        a = jnp.dot(x, w1b_ref[...], preferred_element_type=F32)
        b = jnp.dot(x, w3b_ref[...], preferred_element_type=F32)
        o_ref[...] = (a * jax.nn.sigmoid(a) * b).astype(o_ref.dtype)

    @pl.when(i >= nu_ref[0])
    def _():
        o_ref[...] = jnp.zeros(o_ref.shape, o_ref.dtype)


def _gglu(x, w1, w3, te, nu, *, tm, tn):
    m, k = x.shape
    n = w1.shape[2]
    assert m % tm == 0
    grid = (pl.cdiv(n, tn), m // tm)
    wspec = pl.BlockSpec((None, k, tn), lambda j, i, te, nu: (te[i], 0, j))
    return pl.pallas_call(
        _gglu_body,
        grid_spec=pltpu.PrefetchScalarGridSpec(
            num_scalar_prefetch=2,
            grid=grid,
            in_specs=[pl.BlockSpec((tm, k), lambda j, i, te, nu: (i, 0)), wspec, wspec],
            out_specs=pl.BlockSpec((tm, tn), lambda j, i, te, nu: (i, j)),
            scratch_shapes=[pltpu.VMEM((k, tn), BF16), pltpu.VMEM((k, tn), BF16)],
        ),
        out_shape=jax.ShapeDtypeStruct((m, n), BF16),
        compiler_params=_cparams("parallel", "arbitrary"),
    )(te, nu, x, w1, w3)


def _dense_tiles(m, tm, slab):
    nt = m // tm
    return jnp.full((nt,), slab, jnp.int32), jnp.full((1,), nt, jnp.int32)


def _mm(x, w, slab, *, bias=None, res=None, out_dtype=F32, tm=512, tn=512):
    te, nu = _dense_tiles(x.shape[0], tm, slab)
    return _gmm(x.astype(BF16), w, te, nu, bias=bias, res=res, out_dtype=out_dtype, tm=tm, tn=tn)


def _rms_norm(x, g):
    return x * lax.rsqrt(jnp.mean(x * x, axis=-1, keepdims=True) + NORM_EPS) * g


def _swiglu_res(h, u, w1, w3, w2, layer):
    ub = u.astype(BF16)
    te, nu = _dense_tiles(ub.shape[0], 512, layer)
    act = _gglu(ub, w1, w3, te, nu, tm=512, tn=512)
    return _gmm(act, w2, te, nu, res=h, tm=512, tn=512)


def _moe_res(h, u, router, w1, w3, w2, layer):
    m, d = u.shape
    ts = MOE_ROW_TILE
    w1, w3, w2 = (w.reshape((-1,) + w.shape[2:]) for w in (w1, w3, w2))
    logits = jnp.dot(u, router, precision=lax.Precision.HIGHEST)
    top_val, top_idx = lax.top_k(logits, TOP_K)
    top_w = jax.nn.softmax(top_val, axis=-1)
    flat_e = top_idx.reshape(-1).astype(jnp.int32)
    order = jnp.argsort(flat_e, stable=True).astype(jnp.int32)
    sizes = jnp.zeros((N_EXPERTS,), jnp.int32).at[flat_e].add(1)
    starts = jnp.cumsum(sizes) - sizes
    psizes = ((sizes + ts - 1) // ts) * ts
    pends = jnp.cumsum(psizes)
    pstarts = pends - psizes
    sorted_e = flat_e[order]
    dest = pstarts[sorted_e] + (jnp.arange(m * TOP_K, dtype=jnp.int32) - starts[sorted_e])
    mp = m * TOP_K + N_EXPERTS * ts
    row_token = jnp.zeros((mp,), jnp.int32).at[dest].set(order // TOP_K)
    pos = jnp.zeros((m * TOP_K,), jnp.int32).at[order].set(dest)
    nt = mp // ts
    te = jnp.minimum(jnp.searchsorted(pends, jnp.arange(nt, dtype=jnp.int32) * ts, side="right"),
                     N_EXPERTS - 1).astype(jnp.int32) + layer * N_EXPERTS
    nu = (pends[-1] // ts).astype(jnp.int32).reshape(1)
    xs = jnp.take(u.astype(BF16), row_token, axis=0)
    act = _gglu(xs, w1, w3, te, nu, tm=ts, tn=512)
    ys = _gmm(act, w2, te, nu, tm=ts, tn=256)
    yk = jnp.take(ys, pos, axis=0).reshape(m, TOP_K, d)
    return h + jnp.sum(yk * top_w[..., None], axis=1)


DN_ROW_BLOCK = 256
DN_CONV_HALO = 8


def _bdot(a, b):
    return jnp.dot(a.astype(BF16), b.astype(BF16), preferred_element_type=F32)


def _dn_conv_silu(x, tail_ref, w_ref):
    rows = x.shape[0]
    width = w_ref.shape[0]
    w = w_ref[...]
    xe = jnp.concatenate([tail_ref[...], x], axis=0)
    acc = w[width - 1:width, :] * x
    for k in range(width - 1):
        off = DN_CONV_HALO - (width - 1) + k
        acc = acc + w[k:k + 1, :] * xe[off:off + rows, :]
    tail_ref[...] = x[rows - DN_CONV_HALO:, :]
    return acc * jax.nn.sigmoid(acc)


def _dn_body(q_ref, k_ref, v_ref, z_ref, ba_ref, wq_ref, wk_ref, wv_ref, alog_ref, dt_ref, on_ref, o_ref,
             tq_s, tk_s, tv_s, state_s, gt_s, vnew_s, *, rows):
    hq = pl.program_id(1)
    dh = DN_HEAD_DIM
    c = DN_CHUNK
    nchunk = rows // c
    rep = DN_V_HEADS // DN_QK_HEADS

    @pl.when(pl.program_id(2) == 0)
    def _():
        tq_s[...] = jnp.zeros(tq_s.shape, F32)
        tk_s[...] = jnp.zeros(tk_s.shape, F32)
        tv_s[...] = jnp.zeros(tv_s.shape, F32)
        state_s[...] = jnp.zeros(state_s.shape, F32)

    q = _dn_conv_silu(q_ref[...], tq_s, wq_ref)
    k = _dn_conv_silu(k_ref[...], tk_s, wk_ref)
    v = _dn_conv_silu(v_ref[...], tv_s, wv_ref)
    qn = q * lax.rsqrt(jnp.sum(q * q, axis=-1, keepdims=True) + 1e-6) * (dh ** -0.5)
    kn = k * lax.rsqrt(jnp.sum(k * k, axis=-1, keepdims=True) + 1e-6)

    ba = ba_ref[...]
    beta_all = jax.nn.sigmoid(ba)
    g_all = -jnp.exp(alog_ref[...]) * jax.nn.softplus(ba + dt_ref[...])
    row_in_chunk = lax.broadcasted_iota(jnp.int32, (rows, dh), 0) & (c - 1)
    gc_all = g_all
    shift = 1
    while shift < c:
        gc_all = gc_all + jnp.where(row_in_chunk >= shift, pltpu.roll(gc_all, shift, 0), 0.0)
        shift *= 2
    gt_s[...] = jnp.transpose(gc_all)
    lane = lax.broadcasted_iota(jnp.int32, (rows, dh), 1)

    ri = lax.broadcasted_iota(jnp.int32, (rows, rows), 0)
    ci = lax.broadcasted_iota(jnp.int32, (rows, rows), 1)
    same = (ri // c) == (ci // c)
    lower = same & (ci <= ri)
    strict = same & (ci < ri)
    eye = jnp.where(ri == ci, 1.0, 0.0)
    kk = lax.dot_general(kn.astype(BF16), kn.astype(BF16), _NT, preferred_element_type=F32)
    qk = lax.dot_general(qn.astype(BF16), kn.astype(BF16), _NT, preferred_element_type=F32)

    outs = []
    for e in range(rep):
        hv = rep * hq + e
        beta = jnp.sum(jnp.where(lane == hv, beta_all, 0.0), axis=1, keepdims=True)
        gc = jnp.sum(jnp.where(lane == DN_V_HEADS + hv, gc_all, 0.0), axis=1, keepdims=True)
        gc_row = gt_s[pl.ds(DN_V_HEADS + hv, 1), :]
        decay = jnp.exp(jnp.where(lower, gc - gc_row, NEG_INF))
        a_mat = jnp.where(strict, kk * beta * decay, 0.0)
        m = -a_mat
        p = eye + m
        span = 2
        while span < c:
            m = _bdot(m, m)
            p = p + _bdot(p, m)
            span *= 2
        ve = v[:, e * dh:(e + 1) * dh]
        rhs = jnp.concatenate([ve * beta, kn * (beta * jnp.exp(gc))], axis=1)
        uw = _bdot(p, rhs)
        attn = jnp.where(lower, qk * decay, 0.0).astype(BF16)
        vnew_s[...] = jnp.zeros(vnew_s.shape, BF16)
        o_chunks = []
        for ch in range(nchunk):
            r0 = ch * c
            gcc = gc[r0:r0 + c]
            g_last = gc[r0 + c - 1:r0 + c]
            state = state_s[e]
            wq = jnp.concatenate([uw[r0:r0 + c, dh:], qn[r0:r0 + c] * jnp.exp(gcc)], axis=0)
            wqs = _bdot(wq, state)
            v_new = (uw[r0:r0 + c, :dh] - wqs[:c]).astype(BF16)
            vnew_s[r0:r0 + c, :] = v_new
            o_chunks.append(wqs[c:] + jnp.dot(attn[r0:r0 + c, :], vnew_s[...], preferred_element_type=F32))
            k_dec = (kn[r0:r0 + c] * jnp.exp(g_last - gcc)).astype(BF16)
            state_s[e] = state * jnp.exp(g_last) + lax.dot_general(k_dec, v_new, _TN, preferred_element_type=F32)
        o = jnp.concatenate(o_chunks, axis=0)
        z = z_ref[:, e * dh:(e + 1) * dh]
        o = o * lax.rsqrt(jnp.mean(o * o, axis=-1, keepdims=True) + NORM_EPS) * on_ref[...] * (z * jax.nn.sigmoid(z))
        outs.append(o)
    o_ref[...] = jnp.concatenate(outs, axis=1).astype(o_ref.dtype)


def _dn_mix(y, bsz, seqlen, conv_w, a_log, dt_bias, o_norm):
    dh, hq_n, hv_n = DN_HEAD_DIM, DN_QK_HEADS, DN_V_HEADS
    rep = hv_n // hq_n
    rows = DN_ROW_BLOCK
    assert seqlen % rows == 0 and 2 * hv_n <= dh
    nb = seqlen // rows
    qk_dim, v_dim = hq_n * dh, hv_n * dh
    vb = rep * dh
    pad = jnp.zeros((dh - 2 * hv_n,), F32)
    alog_row = jnp.concatenate([jnp.zeros((hv_n,), F32), a_log, pad]).reshape(1, dh)
    dt_row = jnp.concatenate([jnp.zeros((hv_n,), F32), dt_bias, pad]).reshape(1, dh)
    row = lambda b, h, i: b * nb + i
    return pl.pallas_call(
        functools.partial(_dn_body, rows=rows),
        grid=(bsz, hq_n, nb),
        in_specs=[
            pl.BlockSpec((rows, dh), lambda b, h, i: (row(b, h, i), h)),
            pl.BlockSpec((rows, dh), lambda b, h, i: (row(b, h, i), hq_n + h)),
            pl.BlockSpec((rows, vb), lambda b, h, i: (row(b, h, i), 2 * qk_dim // vb + h)),
            pl.BlockSpec((rows, vb), lambda b, h, i: (row(b, h, i), (2 * qk_dim + v_dim) // vb + h)),
            pl.BlockSpec((rows, dh), lambda b, h, i: (row(b, h, i), (2 * qk_dim + 2 * v_dim) // dh)),
            pl.BlockSpec((conv_w.shape[0], dh), lambda b, h, i: (0, h)),
            pl.BlockSpec((conv_w.shape[0], dh), lambda b, h, i: (0, hq_n + h)),
            pl.BlockSpec((conv_w.shape[0], vb), lambda b, h, i: (0, 2 * qk_dim // vb + h)),
            pl.BlockSpec((1, dh), lambda b, h, i: (0, 0)),
            pl.BlockSpec((1, dh), lambda b, h, i: (0, 0)),
            pl.BlockSpec((1, dh), lambda b, h, i: (0, 0)),
        ],
        out_specs=pl.BlockSpec((rows, vb), lambda b, h, i: (row(b, h, i), h)),
        out_shape=jax.ShapeDtypeStruct((bsz * seqlen, v_dim), BF16),
        scratch_shapes=[
            pltpu.VMEM((DN_CONV_HALO, dh), F32), pltpu.VMEM((DN_CONV_HALO, dh), F32),
            pltpu.VMEM((DN_CONV_HALO, vb), F32), pltpu.VMEM((rep, dh, dh), F32),
            pltpu.VMEM((dh, rows), F32), pltpu.VMEM((rows, dh), BF16),
        ],
        compiler_params=_cparams("parallel", "parallel", "arbitrary"),
    )(y, y, y, y, y, conv_w, conv_w, conv_w, alog_row, dt_row, o_norm.reshape(1, dh))


def _gated_deltanet_res(h, u, bsz, seqlen, j, w_in, conv_w, a_log, dt_bias, o_norm, w_out):
    y = _mm(u, w_in, j, tm=1024, tn=512)
    o = _dn_mix(y, bsz, seqlen, conv_w, a_log, dt_bias, o_norm)
    return _mm(o, w_out, j, res=h, tm=512, tn=512)


def _causal_depthwise_conv(x, w):
    width, ch = w.shape
    return lax.conv_general_dilated(
        x, w[:, None, :], window_strides=(1,), padding=[(width - 1, 0)],
        dimension_numbers=("NWC", "WIO", "NWC"), feature_group_count=ch)


def _conformer_res(h, u, bsz, seqlen, j, w_in, b_in, dw, dw_b, ln_g, ln_b, w_out, b_out):
    y = _mm(u, w_in, j, bias=b_in, tm=1024, tn=512).reshape(bsz, seqlen, -1)
    val, gate = jnp.split(y, 2, axis=-1)
    y = val * jax.nn.sigmoid(gate)
    y = _causal_depthwise_conv(y, dw) + dw_b
    mu = jnp.mean(y, axis=-1, keepdims=True)
    var = jnp.mean(jnp.square(y - mu), axis=-1, keepdims=True)
    y = (y - mu) * lax.rsqrt(var + NORM_EPS) * ln_g + ln_b
    y = jax.nn.silu(y)
    return _mm(y.reshape(bsz * seqlen, -1), w_out, j, bias=b_out, res=h, tm=512, tn=512)


NSA_Q_TILE = 128
NSA_KEY_TILE = 512
NSA_MASK_BIAS = 2.0 ** 60


def _nsa_compress_body(x_ref, w1_ref, w2_ref, o_ref):
    hid = jnp.dot(x_ref[...], w1_ref[...].astype(BF16), preferred_element_type=F32)
    hid = hid * jax.nn.sigmoid(hid)
    o_ref[...] = jnp.dot(hid.astype(BF16), w2_ref[...].astype(BF16), preferred_element_type=F32).astype(o_ref.dtype)


def _nsa_compress(blk, w1, w2, tr):
    two, rows, kdim = blk.shape
    hdim, dh = w2.shape[1], w2.shape[2]
    return pl.pallas_call(
        _nsa_compress_body,
        grid=(two, rows // tr),
        in_specs=[pl.BlockSpec((None, tr, kdim), lambda s, i: (s, i, 0)),
                  pl.BlockSpec((None, kdim, hdim), lambda s, i: (s, 0, 0)),
                  pl.BlockSpec((None, hdim, dh), lambda s, i: (s, 0, 0))],
        out_specs=pl.BlockSpec((None, tr, dh), lambda s, i: (s, i, 0)),
        out_shape=jax.ShapeDtypeStruct((two, rows, dh), BF16),
        compiler_params=_cparams("parallel", "arbitrary"),
    )(blk, w1, w2)


def _softmax_rows(s):
    m = jnp.max(s, axis=-1, keepdims=True)
    e = jnp.exp(s - m)
    return e / jnp.sum(e, axis=-1, keepdims=True)


def _nsa_attn_body(q_ref, ks_ref, vs_ref, kw_ref, vw_ref, gl_ref, kc_ref, vc_ref, cov_ref, o_ref,
                   kaug_s, vs_s, kw_s, vw_s, m_s, l_s, acc_s, *, tq, tk, seqlen):
    qi = pl.program_id(2)
    nr = NSA_HEADS // NSA_KV_GROUPS
    dh = NSA_HEAD_DIM
    rows = nr * tq
    nsel = seqlen // NSA_SEL_BLOCK
    ncmp = seqlen // NSA_CMP_STRIDE

    @pl.when(qi == 0)
    def _():
        row_blk = lax.broadcasted_iota(jnp.int32, (seqlen, dh), 0) // NSA_SEL_BLOCK
        lane = lax.broadcasted_iota(jnp.int32, (seqlen, dh), 1)
        kaug_s[:, :dh] = ks_ref[...].astype(BF16)
        kaug_s[:, dh:] = jnp.where(lane == row_blk, 1.0, 0.0).astype(BF16)
        vs_s[...] = vs_ref[...].astype(BF16)
        kw_s[...] = kw_ref[...].astype(BF16)
        vw_s[...] = vw_ref[...].astype(BF16)

    q = q_ref[...]
    qs = jnp.concatenate([q[:, r * dh:(r + 1) * dh] for r in range(nr)], axis=0) * (dh ** -0.5)
    qb = qs.astype(BF16)
    pos = qi * tq + (lax.broadcasted_iota(jnp.int32, (rows, 1), 0) & (tq - 1))

    s = lax.dot_general(qb, kc_ref[...], _NT, preferred_element_type=F32)
    blk_end = lax.broadcasted_iota(jnp.int32, (rows, ncmp), 1) * NSA_CMP_STRIDE + (NSA_CMP_BLOCK - 1)
    valid = blk_end <= pos
    p = jnp.where(valid, _softmax_rows(jnp.where(valid, s, NEG_INF)), 0.0)
    o_cmp = jnp.dot(p.astype(BF16), vc_ref[...], preferred_element_type=F32)
    psum = p[0:tq] + p[tq:2 * tq] + p[2 * tq:3 * tq] + p[3 * tq:4 * tq]
    p_hi = psum.astype(BF16)
    p_lo = (psum - p_hi.astype(F32)).astype(BF16)
    cov = cov_ref[...]
    imp_t = (lax.dot_general(cov, p_hi, _NT, preferred_element_type=F32)
             + lax.dot_general(cov, p_lo, _NT, preferred_element_type=F32))

    jb = lax.broadcasted_iota(jnp.int32, (nsel, tq), 0)
    post = qi * tq + lax.broadcasted_iota(jnp.int32, (nsel, tq), 1)
    cur = post // NSA_SEL_BLOCK
    forced = (jb == 0) | (jb == cur) | (jb == cur - 1)
    causal = jb * NSA_SEL_BLOCK <= post
    score = jnp.where(forced, NSA_FORCE, jnp.where(causal, imp_t, -NSA_FORCE))
    rank = jnp.zeros((nsel, tq), F32)
    for k in range(nsel):
        rk = score[k:k + 1, :]
        rank = rank + jnp.where(rk > score, 1.0, jnp.where((rk == score) & (jb > k), 1.0, 0.0))
    n_top = min(NSA_TOP_N, nsel)
    bias_t = jnp.where(rank < n_top, 0.0, -NSA_MASK_BIAS)
    bias_t = jnp.concatenate([bias_t, jnp.zeros((dh - nsel, tq), F32)], axis=0)
    bias = jnp.transpose(bias_t).astype(BF16)
    qaug = jnp.concatenate([qb, jnp.concatenate([bias] * nr, axis=0)], axis=1)

    m_s[...] = jnp.full(m_s.shape, NEG_INF, F32)
    l_s[...] = jnp.zeros(l_s.shape, F32)
    acc_s[...] = jnp.zeros(acc_s.shape, F32)

    def sel_step(kt, carry):
        k0 = pl.multiple_of(kt * tk, tk)
        st = lax.dot_general(qaug, kaug_s[pl.ds(k0, tk), :], _NT, preferred_element_type=F32)
        kpos = k0 + lax.broadcasted_iota(jnp.int32, (rows, tk), 1)
        st = jnp.where(kpos <= pos, st, NEG_INF)
        m_prev = m_s[...]
        m_new = jnp.maximum(m_prev, jnp.max(st, axis=-1, keepdims=True))
        alpha = jnp.exp(m_prev - m_new)
        pt = jnp.exp(st - m_new)
        l_s[...] = alpha * l_s[...] + jnp.sum(pt, axis=-1, keepdims=True)
        acc_s[...] = alpha * acc_s[...] + jnp.dot(pt.astype(BF16), vs_s[pl.ds(k0, tk), :],
                                                  preferred_element_type=F32)
        m_s[...] = m_new
        return carry

    lax.fori_loop(0, (qi * tq + tq + tk - 1) // tk, sel_step, 0)
    o_sel = acc_s[...] / l_s[...]

    wk = NSA_WINDOW + tq
    w0 = pl.multiple_of(jnp.maximum(qi * tq - NSA_WINDOW, 0), tq)
    sw = lax.dot_general(qb, kw_s[pl.ds(w0, wk), :], _NT, preferred_element_type=F32)
    diff = pos - (w0 + lax.broadcasted_iota(jnp.int32, (rows, wk), 1))
    pw = _softmax_rows(jnp.where((diff >= 0) & (diff < NSA_WINDOW), sw, NEG_INF))
    o_win = jnp.dot(pw.astype(BF16), vw_s[pl.ds(w0, wk), :], preferred_element_type=F32)

    gate = jax.nn.sigmoid(gl_ref[...])
    outs = []
    for r in range(nr):
        sl = slice(r * tq, (r + 1) * tq)
        outs.append(gate[:, 3 * r:3 * r + 1] * o_cmp[sl] + gate[:, 3 * r + 1:3 * r + 2] * o_sel[sl]
                    + gate[:, 3 * r + 2:3 * r + 3] * o_win[sl])
    o_ref[...] = jnp.concatenate(outs, axis=1).astype(o_ref.dtype)


def _nsa_cover(seqlen):
    n_sel = seqlen // NSA_SEL_BLOCK
    ncmp = seqlen // NSA_CMP_STRIDE
    per = NSA_SEL_BLOCK // NSA_CMP_STRIDE
    cover = np.zeros((n_sel, ncmp), np.float32)
    for j in range(n_sel):
        for m in range(per):
            for n in range(NSA_CMP_BLOCK // NSA_CMP_STRIDE):
                c = per * j + m - n
                if 0 <= c < ncmp - 1:
                    cover[j, c] += 1.0
    return cover


def _nsa_mix(y, bsz, seqlen, pe_k, pe_v, ck_w1, ck_w2, cv_w1, cv_w2):
    ng, nr, dh = NSA_KV_GROUPS, NSA_HEADS // NSA_KV_GROUPS, NSA_HEAD_DIM
    q_dim, kv_dim = NSA_HEADS * dh, ng * dh
    tq, tk = NSA_Q_TILE, NSA_KEY_TILE
    assert seqlen % tk == 0 and seqlen >= NSA_WINDOW + tq and seqlen // NSA_SEL_BLOCK <= dh
    ns = seqlen // NSA_CMP_STRIDE
    y3 = y.reshape(bsz, seqlen, -1)

    def blocks(col0, pe):
        t = y3[:, :, col0:col0 + kv_dim].reshape(bsz, ns, NSA_CMP_STRIDE, ng, dh)
        blk = jnp.concatenate([t[:, 0:ns - 1], t[:, 1:ns]], axis=2) + pe[None, None, :, None, :]
        blk = jnp.moveaxis(blk, 3, 1).reshape(bsz, ng, ns - 1, NSA_CMP_BLOCK * dh)
        blk = jnp.pad(blk, ((0, 0), (0, 0), (0, 1), (0, 0)))
        return blk.reshape(bsz * ng * ns, NSA_CMP_BLOCK * dh).astype(BF16)

    blk = jnp.stack([blocks(q_dim, pe_k), blocks(q_dim + kv_dim, pe_v)])
    cmp = _nsa_compress(blk, jnp.stack([ck_w1, cv_w1]), jnp.stack([ck_w2, cv_w2]), ns)
    cmp = cmp.reshape(2, bsz * ng, ns, dh)
    gl = y3[:, :, q_dim + 6 * kv_dim:].reshape(bsz * seqlen, ng, nr * 3)
    gl = jnp.pad(jnp.moveaxis(gl, 1, 0), ((0, 0), (0, 0), (0, dh - nr * 3)))
    cover = jnp.asarray(_nsa_cover(seqlen), BF16)
    nq = seqlen // tq
    kvc = kv_dim // dh
    c0 = q_dim // dh

    def kv_spec(section):
        return pl.BlockSpec((seqlen, dh), lambda b, g, i, s=section: (b, c0 + s * kvc + g))

    return pl.pallas_call(
        functools.partial(_nsa_attn_body, tq=tq, tk=tk, seqlen=seqlen),
        grid=(bsz, ng, nq),
        in_specs=[
            pl.BlockSpec((tq, nr * dh), lambda b, g, i: (b * nq + i, g)),
            kv_spec(2), kv_spec(3), kv_spec(4), kv_spec(5),
            pl.BlockSpec((None, tq, dh), lambda b, g, i: (g, b * nq + i, 0)),
            pl.BlockSpec((None, None, ns, dh), lambda b, g, i: (0, b * ng + g, 0, 0)),
            pl.BlockSpec((None, None, ns, dh), lambda b, g, i: (1, b * ng + g, 0, 0)),
            pl.BlockSpec((seqlen // NSA_SEL_BLOCK, ns), lambda b, g, i: (0, 0)),
        ],
        out_specs=pl.BlockSpec((tq, nr * dh), lambda b, g, i: (b * nq + i, g)),
        out_shape=jax.ShapeDtypeStruct((bsz * seqlen, q_dim), BF16),
        scratch_shapes=[
            pltpu.VMEM((seqlen, 2 * dh), BF16), pltpu.VMEM((seqlen, dh), BF16),
            pltpu.VMEM((seqlen, dh), BF16), pltpu.VMEM((seqlen, dh), BF16),
            pltpu.VMEM((nr * tq, 1), F32), pltpu.VMEM((nr * tq, 1), F32), pltpu.VMEM((nr * tq, dh), F32),
        ],
        compiler_params=_cparams("parallel", "parallel", "arbitrary"),
    )(y, y, y, y, y, gl, cmp, cmp, cover)


def _nsa_res(h, u, bsz, seqlen, j, w_in, pe_k, pe_v, ck_w1, ck_w2, cv_w1, cv_w2, w_out):
    y = _mm(u, w_in, j, tm=1024, tn=256)
    o = _nsa_mix(y, bsz, seqlen, pe_k, pe_v, ck_w1, ck_w2, cv_w1, cv_w2)
    return _mm(o, w_out, j, res=h, tm=512, tn=512)


def kernel(x, mix_norm, ffn_norm, final_norm, dn_w_in, dn_conv, dn_a_log, dn_dt_bias, dn_o_norm, dn_w_out, cv_w_in, cv_b_in, cv_dw, cv_dw_b, cv_ln_g, cv_ln_b, cv_w_out, cv_b_out, nsa_w_in, nsa_pe_k, nsa_pe_v, nsa_ck_w1, nsa_ck_w2, nsa_cv_w1, nsa_cv_w2, nsa_w_out, ffn_w1, ffn_w3, ffn_w2, moe_router, moe_w1, moe_w3, moe_w2):
    bsz, seqlen, d = x.shape
    depth = mix_norm.shape[0]
    h = x.reshape(bsz * seqlen, d)
    counts = [0, 0, 0]
    for i in range(depth):
        u = _rms_norm(h, mix_norm[i])
        kind = i % 3
        j = counts[kind]
        counts[kind] += 1
        if kind == 0:
            h = _gated_deltanet_res(h, u, bsz, seqlen, j, dn_w_in, dn_conv[j], dn_a_log[j], dn_dt_bias[j],
                                    dn_o_norm[j], dn_w_out)
        elif kind == 1:
            h = _conformer_res(h, u, bsz, seqlen, j, cv_w_in, cv_b_in[j], cv_dw[j], cv_dw_b[j], cv_ln_g[j],
                               cv_ln_b[j], cv_w_out, cv_b_out[j])
        else:
            h = _nsa_res(h, u, bsz, seqlen, j, nsa_w_in, nsa_pe_k[j], nsa_pe_v[j], nsa_ck_w1[j], nsa_ck_w2[j],
                         nsa_cv_w1[j], nsa_cv_w2[j], nsa_w_out)
        u = _rms_norm(h, ffn_norm[i])
        if i % 2 == 0:
            h = _swiglu_res(h, u, ffn_w1, ffn_w3, ffn_w2, i // 2)
        else:
            h = _moe_res(h, u, moe_router[i // 2], moe_w1, moe_w3, moe_w2, i // 2)
    return _rms_norm(h, final_norm).reshape(bsz, seqlen, d)
```

```python
import functools

import numpy as np
import jax
import jax.numpy as jnp
from jax import lax
from jax.experimental import pallas as pl
from jax.experimental.pallas import tpu as pltpu

F32 = jnp.float32
BF16 = jnp.bfloat16

V7X_VMEM_LIMIT_BYTES = 56 * 1024 * 1024

NORM_EPS = 1e-6
NEG_INF = -1e30

DN_QK_HEADS = 16
DN_V_HEADS = 32
DN_HEAD_DIM = 128
DN_CHUNK = 64

NSA_HEADS = 16
NSA_KV_GROUPS = 4
NSA_HEAD_DIM = 128
NSA_CMP_BLOCK = 32
NSA_CMP_STRIDE = 16
NSA_SEL_BLOCK = 64
NSA_TOP_N = 16
NSA_WINDOW = 512
NSA_FORCE = 1e9

N_EXPERTS = 8
TOP_K = 2
MOE_ROW_TILE = 512

_NT = (((1,), (1,)), ((), ()))
_TN = (((0,), (0,)), ((), ()))


def _cparams(*sem):
    return pltpu.CompilerParams(dimension_semantics=sem, vmem_limit_bytes=V7X_VMEM_LIMIT_BYTES)


def _weight_changed(te_ref, i):
    prev = te_ref[jnp.maximum(i - 1, 0)]
    return jnp.logical_or(i == 0, te_ref[i] != prev)


def _gmm_body(te_ref, nu_ref, x_ref, w_ref, *rest, has_bias, has_res):
    rest = list(rest)
    b_ref = rest.pop(0) if has_bias else None
    r_ref = rest.pop(0) if has_res else None
    o_ref, wb_ref = rest
    i = pl.program_id(1)

    @pl.when(_weight_changed(te_ref, i))
    def _():
        wb_ref[...] = w_ref[...].astype(BF16)

    @pl.when(i < nu_ref[0])
    def _():
        acc = jnp.dot(x_ref[...], wb_ref[...], preferred_element_type=F32)
        if has_bias:
            acc = acc + b_ref[...]
        if has_res:
            acc = acc + r_ref[...]
        o_ref[...] = acc.astype(o_ref.dtype)

    @pl.when(i >= nu_ref[0])
    def _():
        o_ref[...] = jnp.zeros(o_ref.shape, o_ref.dtype)


def _gmm(x, w, te, nu, *, bias=None, res=None, out_dtype=F32, tm, tn):
    m, k = x.shape
    n = w.shape[2]
    assert m % tm == 0 and w.shape[1] == k
    grid = (pl.cdiv(n, tn), m // tm)
    in_specs = [
        pl.BlockSpec((tm, k), lambda j, i, te, nu: (i, 0)),
        pl.BlockSpec((None, k, tn), lambda j, i, te, nu: (te[i], 0, j)),
    ]
    args = [x, w]
    if bias is not None:
        in_specs.append(pl.BlockSpec((1, tn), lambda j, i, te, nu: (0, j)))
        args.append(bias.reshape(1, n).astype(F32))
    if res is not None:
        in_specs.append(pl.BlockSpec((tm, tn), lambda j, i, te, nu: (i, j)))
        args.append(res)
    return pl.pallas_call(
        functools.partial(_gmm_body, has_bias=bias is not None, has_res=res is not None),
        grid_spec=pltpu.PrefetchScalarGridSpec(
            num_scalar_prefetch=2,
            grid=grid,
            in_specs=in_specs,
            out_specs=pl.BlockSpec((tm, tn), lambda j, i, te, nu: (i, j)),
            scratch_shapes=[pltpu.VMEM((k, tn), BF16)],
        ),
        out_shape=jax.ShapeDtypeStruct((m, n), out_dtype),
        name="grouped_matmul",
        compiler_params=_cparams("parallel", "arbitrary"),
    )(te, nu, *args)


def _gglu_body(te_ref, nu_ref, x_ref, w1_ref, w3_ref, o_ref, w1b_ref, w3b_ref):
    i = pl.program_id(1)

    @pl.when(_weight_changed(te_ref, i))
    def _():
        w1b_ref[...] = w1_ref[...].astype(BF16)
        w3b_ref[...] = w3_ref[...].astype(BF16)

    @pl.when(i < nu_ref[0])
    def _():
        x = x_ref[...]
        a = jnp.dot(x, w1b_ref[...], preferred_element_type=F32)
        b = jnp.dot(x, w3b_ref[...], preferred_element_type=F32)
        o_ref[...] = (a * jax.nn.sigmoid(a) * b).astype(o_ref.dtype)

    @pl.when(i >= nu_ref[0])
    def _():
        o_ref[...] = jnp.zeros(o_ref.shape, o_ref.dtype)


def _gglu(x, w1, w3, te, nu, *, tm, tn):
    m, k = x.shape
    n = w1.shape[2]
    assert m % tm == 0
    grid = (pl.cdiv(n, tn), m // tm)
    wspec = pl.BlockSpec((None, k, tn), lambda j, i, te, nu: (te[i], 0, j))
    return pl.pallas_call(
        _gglu_body,
        grid_spec=pltpu.PrefetchScalarGridSpec(
            num_scalar_prefetch=2,
            grid=grid,
            in_specs=[pl.BlockSpec((tm, k), lambda j, i, te, nu: (i, 0)), wspec, wspec],
            out_specs=pl.BlockSpec((tm, tn), lambda j, i, te, nu: (i, j)),
            scratch_shapes=[pltpu.VMEM((k, tn), BF16), pltpu.VMEM((k, tn), BF16)],
        ),
        out_shape=jax.ShapeDtypeStruct((m, n), BF16),
        name="grouped_swiglu_up",
        compiler_params=_cparams("parallel", "arbitrary"),
    )(te, nu, x, w1, w3)


def _dense_tiles(m, tm, slab):
    nt = m // tm
    return jnp.full((nt,), slab, jnp.int32), jnp.full((1,), nt, jnp.int32)


def _mm(x, w, slab, *, bias=None, res=None, out_dtype=F32, tm=512, tn=512):
    te, nu = _dense_tiles(x.shape[0], tm, slab)
    return _gmm(x.astype(BF16), w, te, nu, bias=bias, res=res, out_dtype=out_dtype, tm=tm, tn=tn)


def _rms_norm(x, g):
    return x * lax.rsqrt(jnp.mean(x * x, axis=-1, keepdims=True) + NORM_EPS) * g


def _swiglu_res(h, u, w1, w3, w2, layer):
    ub = u.astype(BF16)
    te, nu = _dense_tiles(ub.shape[0], 512, layer)
    act = _gglu(ub, w1, w3, te, nu, tm=512, tn=512)
    return _gmm(act, w2, te, nu, res=h, tm=512, tn=512)


def _moe_res(h, u, router, w1, w3, w2, layer):
    m, d = u.shape
    ts = MOE_ROW_TILE
    w1, w3, w2 = (w.reshape((-1,) + w.shape[2:]) for w in (w1, w3, w2))
    logits = jnp.dot(u, router, precision=lax.Precision.HIGHEST)
    top_val, top_idx = lax.top_k(logits, TOP_K)
    top_w = jax.nn.softmax(top_val, axis=-1)
    flat_e = top_idx.reshape(-1).astype(jnp.int32)
    order = jnp.argsort(flat_e, stable=True).astype(jnp.int32)
    sizes = jnp.zeros((N_EXPERTS,), jnp.int32).at[flat_e].add(1)
    starts = jnp.cumsum(sizes) - sizes
    psizes = ((sizes + ts - 1) // ts) * ts
    pends = jnp.cumsum(psizes)
    pstarts = pends - psizes
    sorted_e = flat_e[order]
    dest = pstarts[sorted_e] + (jnp.arange(m * TOP_K, dtype=jnp.int32) - starts[sorted_e])
    mp = m * TOP_K + N_EXPERTS * ts
    row_token = jnp.zeros((mp,), jnp.int32).at[dest].set(order // TOP_K)
    pos = jnp.zeros((m * TOP_K,), jnp.int32).at[order].set(dest)
    nt = mp // ts
    te = jnp.minimum(jnp.searchsorted(pends, jnp.arange(nt, dtype=jnp.int32) * ts, side="right"),
                     N_EXPERTS - 1).astype(jnp.int32) + layer * N_EXPERTS
    nu = (pends[-1] // ts).astype(jnp.int32).reshape(1)
    xs = jnp.take(u.astype(BF16), row_token, axis=0)
    act = _gglu(xs, w1, w3, te, nu, tm=ts, tn=512)
    ys = _gmm(act, w2, te, nu, out_dtype=BF16, tm=ts, tn=512)
    yk = jnp.take(ys, pos, axis=0).reshape(m, TOP_K, d)
    return h + jnp.sum(yk * top_w[..., None], axis=1)


DN_ROW_BLOCK = 256
DN_HEAD_BLOCK = 2
DN_CONV_HALO = 8


def _bdot(a, b):
    return jnp.dot(a.astype(BF16), b.astype(BF16), preferred_element_type=F32)


def _dn_conv_silu(x_ref, ext_ref, w_ref):
    rows = x_ref.shape[0]
    width = w_ref.shape[0]
    w = w_ref[...]
    ext_ref[DN_CONV_HALO:, :] = x_ref[...]
    acc = w[width - 1:width, :] * x_ref[...]
    for k in range(width - 1):
        off = DN_CONV_HALO - (width - 1) + k
        acc = acc + w[k:k + 1, :] * ext_ref[off:off + rows, :]
    ext_ref[:DN_CONV_HALO, :] = x_ref[rows - DN_CONV_HALO:, :]
    return acc * jax.nn.sigmoid(acc)


def _dn_body(q_ref, k_ref, v_ref, z_ref, ba_ref, wq_ref, wk_ref, wv_ref, alog_ref, dt_ref, on_ref, o_ref,
             tq_s, tk_s, tv_s, state_s, gt_s, *, rows, hq_blk):
    hp = pl.program_id(1)
    dh = DN_HEAD_DIM
    c = DN_CHUNK
    nchunk = rows // c
    rep = DN_V_HEADS // DN_QK_HEADS

    @pl.when(pl.program_id(2) == 0)
    def _():
        tq_s[:DN_CONV_HALO, :] = jnp.zeros((DN_CONV_HALO, tq_s.shape[1]), F32)
        tk_s[:DN_CONV_HALO, :] = jnp.zeros((DN_CONV_HALO, tk_s.shape[1]), F32)
        tv_s[:DN_CONV_HALO, :] = jnp.zeros((DN_CONV_HALO, tv_s.shape[1]), F32)
        state_s[...] = jnp.zeros(state_s.shape, F32)

    q_all = _dn_conv_silu(q_ref, tq_s, wq_ref)
    k_all = _dn_conv_silu(k_ref, tk_s, wk_ref)
    v_all = _dn_conv_silu(v_ref, tv_s, wv_ref)

    ba = ba_ref[...]
    beta_all = jax.nn.sigmoid(ba)
    g_all = -jnp.exp(alog_ref[...]) * jax.nn.softplus(ba + dt_ref[...])
    row_in_chunk = lax.broadcasted_iota(jnp.int32, (rows, dh), 0) & (c - 1)
    gc_all = g_all
    shift = 1
    while shift < c:
        gc_all = gc_all + jnp.where(row_in_chunk >= shift, pltpu.roll(gc_all, shift, 0), 0.0)
        shift *= 2
    gt_s[...] = jnp.transpose(gc_all)
    lane = lax.broadcasted_iota(jnp.int32, (rows, dh), 1)

    ri = lax.broadcasted_iota(jnp.int32, (rows, rows), 0)
    ci = lax.broadcasted_iota(jnp.int32, (rows, rows), 1)
    same = (ri // c) == (ci // c)
    lower = same & (ci <= ri)
    strict = same & (ci < ri)
    eye = jnp.where(ri == ci, 1.0, 0.0)

    outs = []
    for hh in range(hq_blk):
        q = q_all[:, hh * dh:(hh + 1) * dh]
        k = k_all[:, hh * dh:(hh + 1) * dh]
        qn = q * lax.rsqrt(jnp.sum(q * q, axis=-1, keepdims=True) + 1e-6) * (dh ** -0.5)
        kn = k * lax.rsqrt(jnp.sum(k * k, axis=-1, keepdims=True) + 1e-6)
        knb = kn.astype(BF16)
        kk = lax.dot_general(knb, knb, _NT, preferred_element_type=F32)
        qk = lax.dot_general(qn.astype(BF16), knb, _NT, preferred_element_type=F32)
        for e in range(rep):
            sidx = hh * rep + e
            hv = (hp * hq_blk + hh) * rep + e
            beta = jnp.sum(jnp.where(lane == hv, beta_all, 0.0), axis=1, keepdims=True)
            gc = jnp.sum(jnp.where(lane == DN_V_HEADS + hv, gc_all, 0.0), axis=1, keepdims=True)
            gc_row = gt_s[pl.ds(DN_V_HEADS + hv, 1), :]
            decay = jnp.exp(jnp.where(lower, gc - gc_row, NEG_INF))
            a_mat = jnp.where(strict, kk * beta * decay, 0.0)
            m = -a_mat
            p = eye + m
            span = 2
            while span < c:
                m = _bdot(m, m)
                p = p + _bdot(p, m)
                span *= 2
            ve = v_all[:, sidx * dh:(sidx + 1) * dh]
            rhs = jnp.concatenate([ve * beta, kn * (beta * jnp.exp(gc))], axis=1)
            uw = _bdot(p, rhs).astype(BF16)
            attn = jnp.where(lower, qk * decay, 0.0).astype(BF16)
            auw = jnp.dot(attn, uw, preferred_element_type=F32)
            g_last = jnp.concatenate(
                [jnp.broadcast_to(gc[(ch + 1) * c - 1:(ch + 1) * c], (c, 1)) for ch in range(nchunk)], axis=0)
            k_dec = (kn * jnp.exp(g_last - gc)).astype(BF16)
            q_eff = qn * jnp.exp(gc) - auw[:, dh:]
            o_chunks = []
            for ch in range(nchunk):
                r0 = ch * c
                bn = lax.dot_general(k_dec[r0:r0 + c], uw[r0:r0 + c], _TN, preferred_element_type=F32)
                state = state_s[sidx]
                lhs = jnp.concatenate([q_eff[r0:r0 + c], bn[:, dh:]], axis=0)
                prod = _bdot(lhs, state)
                o_chunks.append(prod[:c] + auw[r0:r0 + c, :dh])
                state_s[sidx] = state * jnp.exp(gc[r0 + c - 1:r0 + c]) - prod[c:] + bn[:, :dh]
            o = jnp.concatenate(o_chunks, axis=0)
            z = z_ref[:, sidx * dh:(sidx + 1) * dh]
            o = o * lax.rsqrt(jnp.mean(o * o, axis=-1, keepdims=True) + NORM_EPS) * on_ref[...] * (z * jax.nn.sigmoid(z))
            outs.append(o)
    o_ref[...] = jnp.concatenate(outs, axis=1).astype(o_ref.dtype)


def _dn_mix(y, bsz, seqlen, conv_w, a_log, dt_bias, o_norm):
    dh, hq_n, hv_n = DN_HEAD_DIM, DN_QK_HEADS, DN_V_HEADS
    rep = hv_n // hq_n
    rows, hq_blk = DN_ROW_BLOCK, DN_HEAD_BLOCK
    assert seqlen % rows == 0 and 2 * hv_n <= dh and hq_n % hq_blk == 0
    nb = seqlen // rows
    qk_dim, v_dim = hq_n * dh, hv_n * dh
    qb, vb = hq_blk * dh, hq_blk * rep * dh
    pad = jnp.zeros((dh - 2 * hv_n,), F32)
    alog_row = jnp.concatenate([jnp.zeros((hv_n,), F32), a_log, pad]).reshape(1, dh)
    dt_row = jnp.concatenate([jnp.zeros((hv_n,), F32), dt_bias, pad]).reshape(1, dh)
    taps = conv_w.shape[0]
    return pl.pallas_call(
        functools.partial(_dn_body, rows=rows, hq_blk=hq_blk),
        grid=(bsz, hq_n // hq_blk, nb),
        in_specs=[
            pl.BlockSpec((rows, qb), lambda b, h, i: (b * nb + i, h)),
            pl.BlockSpec((rows, qb), lambda b, h, i: (b * nb + i, qk_dim // qb + h)),
            pl.BlockSpec((rows, vb), lambda b, h, i: (b * nb + i, 2 * qk_dim // vb + h)),
            pl.BlockSpec((rows, vb), lambda b, h, i: (b * nb + i, (2 * qk_dim + v_dim) // vb + h)),
            pl.BlockSpec((rows, dh), lambda b, h, i: (b * nb + i, (2 * qk_dim + 2 * v_dim) // dh)),
            pl.BlockSpec((taps, qb), lambda b, h, i: (0, h)),
            pl.BlockSpec((taps, qb), lambda b, h, i: (0, qk_dim // qb + h)),
            pl.BlockSpec((taps, vb), lambda b, h, i: (0, 2 * qk_dim // vb + h)),
            pl.BlockSpec((1, dh), lambda b, h, i: (0, 0)),
            pl.BlockSpec((1, dh), lambda b, h, i: (0, 0)),
            pl.BlockSpec((1, dh), lambda b, h, i: (0, 0)),
        ],
        out_specs=pl.BlockSpec((rows, vb), lambda b, h, i: (b * nb + i, h)),
        out_shape=jax.ShapeDtypeStruct((bsz * seqlen, v_dim), BF16),
        scratch_shapes=[
            pltpu.VMEM((DN_CONV_HALO + rows, qb), F32), pltpu.VMEM((DN_CONV_HALO + rows, qb), F32),
            pltpu.VMEM((DN_CONV_HALO + rows, vb), F32), pltpu.VMEM((hq_blk * rep, dh, dh), F32),
            pltpu.VMEM((dh, rows), F32),
        ],
        name="dn_chunk_rule",
        compiler_params=_cparams("parallel", "parallel", "arbitrary"),
    )(y, y, y, y, y, conv_w, conv_w, conv_w, alog_row, dt_row, o_norm.reshape(1, dh))


def _gated_deltanet_res(h, u, bsz, seqlen, j, w_in, conv_w, a_log, dt_bias, o_norm, w_out):
    y = _mm(u, w_in, j, tm=1024, tn=512)
    o = _dn_mix(y, bsz, seqlen, conv_w, a_log, dt_bias, o_norm)
    return _mm(o, w_out, j, res=h, tm=512, tn=512)


CV_ROW_BLOCK = 256
CV_CONV_HALO = 32
CV_LANE_CHUNK = 512


def _cv_body(val_ref, gate_ref, dw_ref, dwb_ref, g_ref, b_ref, o_ref, ext_s, conv_s, *, rows):
    width, ch = dw_ref.shape
    halo = CV_CONV_HALO

    @pl.when(pl.program_id(1) == 0)
    def _():
        ext_s[:halo, :] = jnp.zeros((halo, ch), F32)

    gate = gate_ref[...]
    ext_s[halo:, :] = val_ref[...] * jax.nn.sigmoid(gate)
    for c0 in range(0, ch, CV_LANE_CHUNK):
        cs = slice(c0, c0 + CV_LANE_CHUNK)
        acc = jnp.zeros((rows, CV_LANE_CHUNK), F32) + dwb_ref[:, cs]
        for k in range(width):
            off = halo - (width - 1) + k
            acc = acc + dw_ref[k:k + 1, cs] * ext_s[off:off + rows, cs]
        conv_s[:, cs] = acc
    ext_s[:halo, :] = ext_s[rows:rows + halo, :]
    u = conv_s[...]
    mu = jnp.mean(u, axis=-1, keepdims=True)
    d = u - mu
    var = jnp.mean(d * d, axis=-1, keepdims=True)
    u = d * lax.rsqrt(var + NORM_EPS) * g_ref[...] + b_ref[...]
    o_ref[...] = (u * jax.nn.sigmoid(u)).astype(o_ref.dtype)


def _cv_mix(y, bsz, seqlen, dw, dw_b, ln_g, ln_b):
    rows = CV_ROW_BLOCK
    width, ch = dw.shape
    assert seqlen % rows == 0 and width - 1 <= CV_CONV_HALO <= rows and ch % CV_LANE_CHUNK == 0
    nb = seqlen // rows
    vec = lambda v: v.reshape(1, ch)
    vspec = pl.BlockSpec((1, ch), lambda b, i: (0, 0))
    return pl.pallas_call(
        functools.partial(_cv_body, rows=rows),
        grid=(bsz, nb),
        in_specs=[
            pl.BlockSpec((rows, ch), lambda b, i: (b * nb + i, 0)),
            pl.BlockSpec((rows, ch), lambda b, i: (b * nb + i, 1)),
            pl.BlockSpec((width, ch), lambda b, i: (0, 0)),
            vspec, vspec, vspec,
        ],
        out_specs=pl.BlockSpec((rows, ch), lambda b, i: (b * nb + i, 0)),
        out_shape=jax.ShapeDtypeStruct((bsz * seqlen, ch), BF16),
        scratch_shapes=[pltpu.VMEM((CV_CONV_HALO + rows, ch), F32), pltpu.VMEM((rows, ch), F32)],
        name="conformer_conv",
        compiler_params=_cparams("parallel", "arbitrary"),
    )(y, y, dw, vec(dw_b), vec(ln_g), vec(ln_b))


def _conformer_res(h, u, bsz, seqlen, j, w_in, b_in, dw, dw_b, ln_g, ln_b, w_out, b_out):
    y = _mm(u, w_in, j, bias=b_in, tm=1024, tn=512)
    a = _cv_mix(y, bsz, seqlen, dw, dw_b, ln_g, ln_b)
    return _mm(a, w_out, j, bias=b_out, res=h, tm=512, tn=512)


NSA_Q_TILE = 128
NSA_KEY_TILE = 512
NSA_MASK_BIAS = 2.0 ** 60


def _nsa_compress_body(x_ref, w1_ref, w2_ref, o_ref):
    hid = jnp.dot(x_ref[...], w1_ref[...].astype(BF16), preferred_element_type=F32)
    hid = hid * jax.nn.sigmoid(hid)
    o_ref[...] = jnp.dot(hid.astype(BF16), w2_ref[...].astype(BF16), preferred_element_type=F32).astype(o_ref.dtype)


def _nsa_compress(blk, w1, w2, tr):
    two, rows, kdim = blk.shape
    hdim, dh = w2.shape[1], w2.shape[2]
    return pl.pallas_call(
        _nsa_compress_body,
        grid=(two, rows // tr),
        in_specs=[pl.BlockSpec((None, tr, kdim), lambda s, i: (s, i, 0)),
                  pl.BlockSpec((None, kdim, hdim), lambda s, i: (s, 0, 0)),
                  pl.BlockSpec((None, hdim, dh), lambda s, i: (s, 0, 0))],
        out_specs=pl.BlockSpec((None, tr, dh), lambda s, i: (s, i, 0)),
        out_shape=jax.ShapeDtypeStruct((two, rows, dh), BF16),
        name="nsa_compress",
        compiler_params=_cparams("parallel", "arbitrary"),
    )(blk, w1, w2)


def _softmax_rows(s):
    m = jnp.max(s, axis=-1, keepdims=True)
    e = jnp.exp(s - m)
    return e / jnp.sum(e, axis=-1, keepdims=True)


def _nsa_attn_body(q_ref, ks_ref, vs_ref, kw_ref, vw_ref, gl_ref, kc_ref, vc_ref, cov_ref, o_ref,
                   kaug_s, vs_s, kw_s, vw_s, m_s, l_s, acc_s, *, tq, tk, seqlen):
    qi = pl.program_id(2)
    nr = NSA_HEADS // NSA_KV_GROUPS
    dh = NSA_HEAD_DIM
    rows = nr * tq
    nsel = seqlen // NSA_SEL_BLOCK
    ncmp = seqlen // NSA_CMP_STRIDE

    @pl.when(qi == 0)
    def _():
        row_blk = lax.broadcasted_iota(jnp.int32, (seqlen, dh), 0) // NSA_SEL_BLOCK
        lane = lax.broadcasted_iota(jnp.int32, (seqlen, dh), 1)
        kaug_s[:, :dh] = ks_ref[...].astype(BF16)
        kaug_s[:, dh:] = jnp.where(lane == row_blk, 1.0, 0.0).astype(BF16)
        vs_s[...] = vs_ref[...].astype(BF16)
        kw_s[...] = kw_ref[...].astype(BF16)
        vw_s[...] = vw_ref[...].astype(BF16)

    q = q_ref[...]
    qs = jnp.concatenate([q[:, r * dh:(r + 1) * dh] for r in range(nr)], axis=0) * (dh ** -0.5)
    qb = qs.astype(BF16)
    pos = qi * tq + (lax.broadcasted_iota(jnp.int32, (rows, 1), 0) & (tq - 1))

    s = lax.dot_general(qb, kc_ref[...], _NT, preferred_element_type=F32)
    blk_end = lax.broadcasted_iota(jnp.int32, (rows, ncmp), 1) * NSA_CMP_STRIDE + (NSA_CMP_BLOCK - 1)
    valid = blk_end <= pos
    p = jnp.where(valid, _softmax_rows(jnp.where(valid, s, NEG_INF)), 0.0)
    o_cmp = jnp.dot(p.astype(BF16), vc_ref[...], preferred_element_type=F32)
    psum = p[0:tq] + p[tq:2 * tq] + p[2 * tq:3 * tq] + p[3 * tq:4 * tq]
    p_hi = psum.astype(BF16)
    p_lo = (psum - p_hi.astype(F32)).astype(BF16)
    cov = cov_ref[...]
    imp_t = (lax.dot_general(cov, p_hi, _NT, preferred_element_type=F32)
             + lax.dot_general(cov, p_lo, _NT, preferred_element_type=F32))

    jb = lax.broadcasted_iota(jnp.int32, (nsel, tq), 0)
    post = qi * tq + lax.broadcasted_iota(jnp.int32, (nsel, tq), 1)
    cur = post // NSA_SEL_BLOCK
    forced = (jb == 0) | (jb == cur) | (jb == cur - 1)
    causal = jb * NSA_SEL_BLOCK <= post
    score = jnp.where(forced, NSA_FORCE, jnp.where(causal, imp_t, -NSA_FORCE))
    rank = jnp.zeros((nsel, tq), F32)
    for k in range(nsel):
        rk = score[k:k + 1, :]
        rank = rank + jnp.where(rk > score, 1.0, jnp.where((rk == score) & (jb > k), 1.0, 0.0))
    n_top = min(NSA_TOP_N, nsel)
    bias_t = jnp.where(rank < n_top, 0.0, -NSA_MASK_BIAS)
    bias_t = jnp.concatenate([bias_t, jnp.zeros((dh - nsel, tq), F32)], axis=0)
    bias = jnp.transpose(bias_t).astype(BF16)
    qaug = jnp.concatenate([qb, jnp.concatenate([bias] * nr, axis=0)], axis=1)

    m_s[...] = jnp.full(m_s.shape, NEG_INF, F32)
    l_s[...] = jnp.zeros(l_s.shape, F32)
    acc_s[...] = jnp.zeros(acc_s.shape, F32)

    def sel_step(kt, carry, causal_mask):
        k0 = pl.multiple_of(kt * tk, tk)
        st = lax.dot_general(qaug, kaug_s[pl.ds(k0, tk), :], _NT, preferred_element_type=F32)
        if causal_mask:
            kpos = k0 + lax.broadcasted_iota(jnp.int32, (rows, tk), 1)
            st = jnp.where(kpos <= pos, st, NEG_INF)
        m_prev = m_s[...]
        m_new = jnp.maximum(m_prev, jnp.max(st, axis=-1, keepdims=True))
        alpha = jnp.exp(m_prev - m_new)
        pt = jnp.exp(st - m_new)
        l_s[...] = alpha * l_s[...] + jnp.sum(pt, axis=-1, keepdims=True)
        acc_s[...] = alpha * acc_s[...] + jnp.dot(pt.astype(BF16), vs_s[pl.ds(k0, tk), :],
                                                  preferred_element_type=F32)
        m_s[...] = m_new
        return carry

    n_full = (qi * tq) // tk
    lax.fori_loop(0, n_full, functools.partial(sel_step, causal_mask=False), 0)
    sel_step(n_full, 0, causal_mask=True)
    o_sel = acc_s[...] / l_s[...]

    wk = NSA_WINDOW + tq
    w0 = pl.multiple_of(jnp.maximum(qi * tq - NSA_WINDOW, 0), tq)
    sw = lax.dot_general(qb, kw_s[pl.ds(w0, wk), :], _NT, preferred_element_type=F32)
    diff = pos - (w0 + lax.broadcasted_iota(jnp.int32, (rows, wk), 1))
    pw = _softmax_rows(jnp.where((diff >= 0) & (diff < NSA_WINDOW), sw, NEG_INF))
    o_win = jnp.dot(pw.astype(BF16), vw_s[pl.ds(w0, wk), :], preferred_element_type=F32)

    gate = jax.nn.sigmoid(gl_ref[...])
    outs = []
    for r in range(nr):
        sl = slice(r * tq, (r + 1) * tq)
        outs.append(gate[:, 3 * r:3 * r + 1] * o_cmp[sl] + gate[:, 3 * r + 1:3 * r + 2] * o_sel[sl]
                    + gate[:, 3 * r + 2:3 * r + 3] * o_win[sl])
    o_ref[...] = jnp.concatenate(outs, axis=1).astype(o_ref.dtype)


def _nsa_cover(seqlen):
    n_sel = seqlen // NSA_SEL_BLOCK
    ncmp = seqlen // NSA_CMP_STRIDE
    per = NSA_SEL_BLOCK // NSA_CMP_STRIDE
    cover = np.zeros((n_sel, ncmp), np.float32)
    for j in range(n_sel):
        for m in range(per):
            for n in range(NSA_CMP_BLOCK // NSA_CMP_STRIDE):
                c = per * j + m - n
                if 0 <= c < ncmp - 1:
                    cover[j, c] += 1.0
    return cover


def _nsa_mix(y, bsz, seqlen, pe_k, pe_v, ck_w1, ck_w2, cv_w1, cv_w2):
    ng, nr, dh = NSA_KV_GROUPS, NSA_HEADS // NSA_KV_GROUPS, NSA_HEAD_DIM
    q_dim, kv_dim = NSA_HEADS * dh, ng * dh
    tq, tk = NSA_Q_TILE, NSA_KEY_TILE
    assert seqlen % tk == 0 and tk % tq == 0 and seqlen >= NSA_WINDOW + tq and seqlen // NSA_SEL_BLOCK <= dh
    ns = seqlen // NSA_CMP_STRIDE
    y3 = y.reshape(bsz, seqlen, -1)

    def blocks(col0, pe):
        t = y3[:, :, col0:col0 + kv_dim].reshape(bsz, ns, NSA_CMP_STRIDE, ng, dh)
        blk = jnp.concatenate([t[:, 0:ns - 1], t[:, 1:ns]], axis=2) + pe[None, None, :, None, :]
        blk = jnp.moveaxis(blk, 3, 1).reshape(bsz, ng, ns - 1, NSA_CMP_BLOCK * dh)
        blk = jnp.pad(blk, ((0, 0), (0, 0), (0, 1), (0, 0)))
        return blk.reshape(bsz * ng * ns, NSA_CMP_BLOCK * dh).astype(BF16)

    blk = jnp.stack([blocks(q_dim, pe_k), blocks(q_dim + kv_dim, pe_v)])
    cmp = _nsa_compress(blk, jnp.stack([ck_w1, cv_w1]), jnp.stack([ck_w2, cv_w2]), ns)
    cmp = cmp.reshape(2, bsz * ng, ns, dh)
    gl = y3[:, :, q_dim + 6 * kv_dim:].reshape(bsz * seqlen, ng, nr * 3)
    gl = jnp.pad(jnp.moveaxis(gl, 1, 0), ((0, 0), (0, 0), (0, dh - nr * 3)))
    cover = jnp.asarray(_nsa_cover(seqlen), BF16)
    nq = seqlen // tq
    kvc = kv_dim // dh
    c0 = q_dim // dh

    def kv_spec(section):
        return pl.BlockSpec((seqlen, dh), lambda b, g, i, s=section: (b, c0 + s * kvc + g))

    return pl.pallas_call(
        functools.partial(_nsa_attn_body, tq=tq, tk=tk, seqlen=seqlen),
        grid=(bsz, ng, nq),
        in_specs=[
            pl.BlockSpec((tq, nr * dh), lambda b, g, i: (b * nq + i, g)),
            kv_spec(2), kv_spec(3), kv_spec(4), kv_spec(5),
            pl.BlockSpec((None, tq, dh), lambda b, g, i: (g, b * nq + i, 0)),
            pl.BlockSpec((None, None, ns, dh), lambda b, g, i: (0, b * ng + g, 0, 0)),
            pl.BlockSpec((None, None, ns, dh), lambda b, g, i: (1, b * ng + g, 0, 0)),
            pl.BlockSpec((seqlen // NSA_SEL_BLOCK, ns), lambda b, g, i: (0, 0)),
        ],
        out_specs=pl.BlockSpec((tq, nr * dh), lambda b, g, i: (b * nq + i, g)),
        out_shape=jax.ShapeDtypeStruct((bsz * seqlen, q_dim), BF16),
        scratch_shapes=[
            pltpu.VMEM((seqlen, 2 * dh), BF16), pltpu.VMEM((seqlen, dh), BF16),
            pltpu.VMEM((seqlen, dh), BF16), pltpu.VMEM((seqlen, dh), BF16),
            pltpu.VMEM((nr * tq, 1), F32), pltpu.VMEM((nr * tq, 1), F32), pltpu.VMEM((nr * tq, dh), F32),
        ],
        name="nsa_attention",
        compiler_params=_cparams("parallel", "parallel", "arbitrary"),
    )(y, y, y, y, y, gl, cmp, cmp, cover)


def _nsa_res(h, u, bsz, seqlen, j, w_in, pe_k, pe_v, ck_w1, ck_w2, cv_w1, cv_w2, w_out):
    y = _mm(u, w_in, j, tm=1024, tn=256)
    o = _nsa_mix(y, bsz, seqlen, pe_k, pe_v, ck_w1, ck_w2, cv_w1, cv_w2)
    return _mm(o, w_out, j, res=h, tm=512, tn=512)


def kernel(x, mix_norm, ffn_norm, final_norm, dn_w_in, dn_conv, dn_a_log, dn_dt_bias, dn_o_norm, dn_w_out, cv_w_in, cv_b_in, cv_dw, cv_dw_b, cv_ln_g, cv_ln_b, cv_w_out, cv_b_out, nsa_w_in, nsa_pe_k, nsa_pe_v, nsa_ck_w1, nsa_ck_w2, nsa_cv_w1, nsa_cv_w2, nsa_w_out, ffn_w1, ffn_w3, ffn_w2, moe_router, moe_w1, moe_w3, moe_w2):
    bsz, seqlen, d = x.shape
    depth = mix_norm.shape[0]
    h = x.reshape(bsz * seqlen, d)
    counts = [0, 0, 0]
    for i in range(depth):
        u = _rms_norm(h, mix_norm[i])
        kind = i % 3
        j = counts[kind]
        counts[kind] += 1
        if kind == 0:
            h = _gated_deltanet_res(h, u, bsz, seqlen, j, dn_w_in, dn_conv[j], dn_a_log[j], dn_dt_bias[j],
                                    dn_o_norm[j], dn_w_out)
        elif kind == 1:
            h = _conformer_res(h, u, bsz, seqlen, j, cv_w_in, cv_b_in[j], cv_dw[j], cv_dw_b[j], cv_ln_g[j],
                               cv_ln_b[j], cv_w_out, cv_b_out[j])
        else:
            h = _nsa_res(h, u, bsz, seqlen, j, nsa_w_in, nsa_pe_k[j], nsa_pe_v[j], nsa_ck_w1[j], nsa_ck_w2[j],
                         nsa_cv_w1[j], nsa_cv_w2[j], nsa_w_out)
        u = _rms_norm(h, ffn_norm[i])
        if i % 2 == 0:
            h = _swiglu_res(h, u, ffn_w1, ffn_w3, ffn_w2, i // 2)
        else:
            h = _moe_res(h, u, moe_router[i // 2], moe_w1, moe_w3, moe_w2, i // 2)
    return _rms_norm(h, final_norm).reshape(bsz, seqlen, d)
```

```python
import functools

import numpy as np
import jax
import jax.numpy as jnp
from jax import lax
from jax.experimental import pallas as pl
from jax.experimental.pallas import tpu as pltpu

F32 = jnp.float32
BF16 = jnp.bfloat16

V7X_VMEM_LIMIT_BYTES = 56 * 1024 * 1024

NORM_EPS = 1e-6
NEG_INF = -1e30

DN_QK_HEADS = 16
DN_V_HEADS = 32
DN_HEAD_DIM = 128
DN_CHUNK = 64

NSA_HEADS = 16
NSA_KV_GROUPS = 4
NSA_HEAD_DIM = 128
NSA_CMP_BLOCK = 32
NSA_CMP_STRIDE = 16
NSA_SEL_BLOCK = 64
NSA_TOP_N = 16
NSA_WINDOW = 512
NSA_FORCE = 1e9

N_EXPERTS = 8
TOP_K = 2
MOE_ROW_TILE = 512

_NT = (((1,), (1,)), ((), ()))
_TN = (((0,), (0,)), ((), ()))


def _cparams(*sem):
    return pltpu.CompilerParams(dimension_semantics=sem, vmem_limit_bytes=V7X_VMEM_LIMIT_BYTES)


MM_SUB_TILE = 512


def _sub_tiles(shape):
    tm, tn = shape
    sm, sn = min(tm, MM_SUB_TILE), min(tn, MM_SUB_TILE)
    return [(slice(r, r + sm), slice(c, c + sn)) for r in range(0, tm, sm) for c in range(0, tn, sn)]


def _weight_changed(te_ref, i):
    prev = te_ref[jnp.maximum(i - 1, 0)]
    return jnp.logical_or(i == 0, te_ref[i] != prev)


def _gmm_body(te_ref, nu_ref, x_ref, w_ref, *rest, has_bias, has_res):
    rest = list(rest)
    b_ref = rest.pop(0) if has_bias else None
    r_ref = rest.pop(0) if has_res else None
    o_ref, wb_ref = rest
    i = pl.program_id(1)

    @pl.when(_weight_changed(te_ref, i))
    def _():
        wb_ref[...] = w_ref[...].astype(BF16)

    @pl.when(i < nu_ref[0])
    def _():
        for rs, cs in _sub_tiles(o_ref.shape):
            acc = jnp.dot(x_ref[rs, :], wb_ref[:, cs], preferred_element_type=F32)
            if has_bias:
                acc = acc + b_ref[:, cs]
            if has_res:
                acc = acc + r_ref[rs, cs]
            o_ref[rs, cs] = acc.astype(o_ref.dtype)

    @pl.when(i >= nu_ref[0])
    def _():
        o_ref[...] = jnp.zeros(o_ref.shape, o_ref.dtype)


def _gmm(x, w, te, nu, *, bias=None, res=None, out_dtype=F32, tm, tn):
    m, k = x.shape
    n = w.shape[2]
    assert m % tm == 0 and w.shape[1] == k
    grid = (pl.cdiv(n, tn), m // tm)
    in_specs = [
        pl.BlockSpec((tm, k), lambda j, i, te, nu: (i, 0)),
        pl.BlockSpec((None, k, tn), lambda j, i, te, nu: (te[i], 0, j)),
    ]
    args = [x, w]
    if bias is not None:
        in_specs.append(pl.BlockSpec((1, tn), lambda j, i, te, nu: (0, j)))
        args.append(bias.reshape(1, n).astype(F32))
    if res is not None:
        in_specs.append(pl.BlockSpec((tm, tn), lambda j, i, te, nu: (i, j)))
        args.append(res)
    return pl.pallas_call(
        functools.partial(_gmm_body, has_bias=bias is not None, has_res=res is not None),
        grid_spec=pltpu.PrefetchScalarGridSpec(
            num_scalar_prefetch=2,
            grid=grid,
            in_specs=in_specs,
            out_specs=pl.BlockSpec((tm, tn), lambda j, i, te, nu: (i, j)),
            scratch_shapes=[pltpu.VMEM((k, tn), BF16)],
        ),
        out_shape=jax.ShapeDtypeStruct((m, n), out_dtype),
        name="grouped_matmul",
        compiler_params=_cparams("parallel", "arbitrary"),
    )(te, nu, *args)


def _gglu_body(te_ref, nu_ref, x_ref, w1_ref, w3_ref, o_ref, w1b_ref, w3b_ref):
    i = pl.program_id(1)

    @pl.when(_weight_changed(te_ref, i))
    def _():
        w1b_ref[...] = w1_ref[...].astype(BF16)
        w3b_ref[...] = w3_ref[...].astype(BF16)

    @pl.when(i < nu_ref[0])
    def _():
        for rs, cs in _sub_tiles(o_ref.shape):
            x = x_ref[rs, :]
            a = jnp.dot(x, w1b_ref[:, cs], preferred_element_type=F32)
            b = jnp.dot(x, w3b_ref[:, cs], preferred_element_type=F32)
            o_ref[rs, cs] = (a * jax.nn.sigmoid(a) * b).astype(o_ref.dtype)

    @pl.when(i >= nu_ref[0])
    def _():
        o_ref[...] = jnp.zeros(o_ref.shape, o_ref.dtype)


def _gglu(x, w1, w3, te, nu, *, tm, tn):
    m, k = x.shape
    n = w1.shape[2]
    assert m % tm == 0
    grid = (pl.cdiv(n, tn), m // tm)
    wspec = pl.BlockSpec((None, k, tn), lambda j, i, te, nu: (te[i], 0, j))
    return pl.pallas_call(
        _gglu_body,
        grid_spec=pltpu.PrefetchScalarGridSpec(
            num_scalar_prefetch=2,
            grid=grid,
            in_specs=[pl.BlockSpec((tm, k), lambda j, i, te, nu: (i, 0)), wspec, wspec],
            out_specs=pl.BlockSpec((tm, tn), lambda j, i, te, nu: (i, j)),
            scratch_shapes=[pltpu.VMEM((k, tn), BF16), pltpu.VMEM((k, tn), BF16)],
        ),
        out_shape=jax.ShapeDtypeStruct((m, n), BF16),
        name="grouped_swiglu_up",
        compiler_params=_cparams("parallel", "arbitrary"),
    )(te, nu, x, w1, w3)


def _dense_tiles(m, tm, slab):
    nt = m // tm
    return jnp.full((nt,), slab, jnp.int32), jnp.full((1,), nt, jnp.int32)


def _mm(x, w, slab, *, bias=None, res=None, out_dtype=F32, tm=512, tn=512):
    te, nu = _dense_tiles(x.shape[0], tm, slab)
    return _gmm(x.astype(BF16), w, te, nu, bias=bias, res=res, out_dtype=out_dtype, tm=tm, tn=tn)


def _rms_norm(x, g):
    return x * lax.rsqrt(jnp.mean(x * x, axis=-1, keepdims=True) + NORM_EPS) * g


def _swiglu_res(h, u, w1, w3, w2, layer):
    ub = u.astype(BF16)
    te, nu = _dense_tiles(ub.shape[0], 1024, layer)
    act = _gglu(ub, w1, w3, te, nu, tm=1024, tn=512)
    te, nu = _dense_tiles(ub.shape[0], 512, layer)
    return _gmm(act, w2, te, nu, res=h, tm=512, tn=512)


MOE_COMBINE_TILE = 256
MOE_GATE_LANES = 128


def _moe_combine_body(pos_ref, h_ref, w_ref, ys_hbm, o_ref, buf, sem, *, tt):
    base = pl.program_id(0) * (tt * TOP_K)

    def row_copy(slot, src_row):
        return pltpu.make_async_copy(ys_hbm.at[pl.ds(src_row, 1), :], buf.at[pl.ds(slot, 1), :], sem.at[0])

    def issue(t, carry):
        for k in range(TOP_K):
            row_copy(k * tt + t, pos_ref[base + t * TOP_K + k]).start()
        return carry

    lax.fori_loop(0, tt, issue, 0)

    def drain(slot, carry):
        row_copy(slot, 0).wait()
        return carry

    lax.fori_loop(0, tt * TOP_K, drain, 0)
    acc = h_ref[...]
    w = w_ref[...]
    for k in range(TOP_K):
        acc = acc + w[:, k:k + 1] * buf[k * tt:(k + 1) * tt, :]
    o_ref[...] = acc


def _moe_combine(h, ys, pos, top_w):
    m, d = h.shape
    tt = MOE_COMBINE_TILE
    assert m % tt == 0
    w = jnp.pad(top_w.astype(F32), ((0, 0), (0, MOE_GATE_LANES - TOP_K)))
    return pl.pallas_call(
        functools.partial(_moe_combine_body, tt=tt),
        grid_spec=pltpu.PrefetchScalarGridSpec(
            num_scalar_prefetch=1,
            grid=(m // tt,),
            in_specs=[pl.BlockSpec((tt, d), lambda i, pos: (i, 0)),
                      pl.BlockSpec((tt, MOE_GATE_LANES), lambda i, pos: (i, 0)),
                      pl.BlockSpec(memory_space=pl.ANY)],
            out_specs=pl.BlockSpec((tt, d), lambda i, pos: (i, 0)),
            scratch_shapes=[pltpu.VMEM((TOP_K * tt, d), F32), pltpu.SemaphoreType.DMA((1,))],
        ),
        out_shape=jax.ShapeDtypeStruct((m, d), F32),
        name="moe_combine",
        compiler_params=_cparams("arbitrary"),
    )(pos, h, w, ys)


def _moe_dispatch_body(tok_ref, u_hbm, o_ref, buf, sem, *, tr):
    base = pl.program_id(0) * tr

    def row_copy(slot, src_row):
        return pltpu.make_async_copy(u_hbm.at[pl.ds(src_row, 1), :], buf.at[pl.ds(slot, 1), :], sem.at[0])

    def issue(r, carry):
        row_copy(r, tok_ref[base + r]).start()
        return carry

    lax.fori_loop(0, tr, issue, 0)

    def drain(slot, carry):
        row_copy(slot, 0).wait()
        return carry

    lax.fori_loop(0, tr, drain, 0)
    o_ref[...] = buf[...].astype(o_ref.dtype)


def _moe_dispatch(u, row_token):
    d = u.shape[1]
    rows = row_token.shape[0]
    tr = MOE_COMBINE_TILE
    assert rows % tr == 0
    return pl.pallas_call(
        functools.partial(_moe_dispatch_body, tr=tr),
        grid_spec=pltpu.PrefetchScalarGridSpec(
            num_scalar_prefetch=1,
            grid=(rows // tr,),
            in_specs=[pl.BlockSpec(memory_space=pl.ANY)],
            out_specs=pl.BlockSpec((tr, d), lambda i, tok: (i, 0)),
            scratch_shapes=[pltpu.VMEM((tr, d), F32), pltpu.SemaphoreType.DMA((1,))],
        ),
        out_shape=jax.ShapeDtypeStruct((rows, d), BF16),
        name="moe_dispatch",
        compiler_params=_cparams("arbitrary"),
    )(row_token, u)


def _moe_res(h, u, router, w1, w3, w2, layer):
    m, d = u.shape
    ts = MOE_ROW_TILE
    w1, w3, w2 = (w.reshape((-1,) + w.shape[2:]) for w in (w1, w3, w2))
    logits = jnp.dot(u, router, precision=lax.Precision.HIGHEST)
    top_val, top_idx = lax.top_k(logits, TOP_K)
    top_w = jax.nn.softmax(top_val, axis=-1)
    flat_e = top_idx.reshape(-1).astype(jnp.int32)
    order = jnp.argsort(flat_e, stable=True).astype(jnp.int32)
    sizes = jnp.zeros((N_EXPERTS,), jnp.int32).at[flat_e].add(1)
    starts = jnp.cumsum(sizes) - sizes
    psizes = ((sizes + ts - 1) // ts) * ts
    pends = jnp.cumsum(psizes)
    pstarts = pends - psizes
    sorted_e = flat_e[order]
    dest = pstarts[sorted_e] + (jnp.arange(m * TOP_K, dtype=jnp.int32) - starts[sorted_e])
    mp = m * TOP_K + N_EXPERTS * ts
    row_token = jnp.zeros((mp,), jnp.int32).at[dest].set(order // TOP_K)
    pos = jnp.zeros((m * TOP_K,), jnp.int32).at[order].set(dest)
    nt = mp // ts
    te = jnp.minimum(jnp.searchsorted(pends, jnp.arange(nt, dtype=jnp.int32) * ts, side="right"),
                     N_EXPERTS - 1).astype(jnp.int32) + layer * N_EXPERTS
    nu = (pends[-1] // ts).astype(jnp.int32).reshape(1)
    xs = _moe_dispatch(u, row_token)
    act = _gglu(xs, w1, w3, te, nu, tm=ts, tn=1024)
    ys = _gmm(act, w2, te, nu, tm=ts, tn=512)
    return _moe_combine(h, ys, pos, top_w)


DN_ROW_BLOCK = 256
DN_HEAD_BLOCK = 2
DN_CONV_HALO = 8


def _bdot(a, b):
    return jnp.dot(a.astype(BF16), b.astype(BF16), preferred_element_type=F32)


def _dn_conv_silu(x_ref, ext_ref, w_ref):
    rows = x_ref.shape[0]
    width = w_ref.shape[0]
    w = w_ref[...]
    ext_ref[DN_CONV_HALO:, :] = x_ref[...]
    acc = w[width - 1:width, :] * x_ref[...]
    for k in range(width - 1):
        off = DN_CONV_HALO - (width - 1) + k
        acc = acc + w[k:k + 1, :] * ext_ref[off:off + rows, :]
    ext_ref[:DN_CONV_HALO, :] = x_ref[rows - DN_CONV_HALO:, :]
    return acc * jax.nn.sigmoid(acc)


def _dn_body(q_ref, k_ref, v_ref, z_ref, ba_ref, wq_ref, wk_ref, wv_ref, alog_ref, dt_ref, on_ref, o_ref,
             tq_s, tk_s, tv_s, state_s, gt_s, *, rows, hq_blk):
    hp = pl.program_id(1)
    dh = DN_HEAD_DIM
    c = DN_CHUNK
    nchunk = rows // c
    rep = DN_V_HEADS // DN_QK_HEADS

    @pl.when(pl.program_id(2) == 0)
    def _():
        tq_s[:DN_CONV_HALO, :] = jnp.zeros((DN_CONV_HALO, tq_s.shape[1]), F32)
        tk_s[:DN_CONV_HALO, :] = jnp.zeros((DN_CONV_HALO, tk_s.shape[1]), F32)
        tv_s[:DN_CONV_HALO, :] = jnp.zeros((DN_CONV_HALO, tv_s.shape[1]), F32)
        state_s[...] = jnp.zeros(state_s.shape, F32)

    q_all = _dn_conv_silu(q_ref, tq_s, wq_ref)
    k_all = _dn_conv_silu(k_ref, tk_s, wk_ref)
    v_all = _dn_conv_silu(v_ref, tv_s, wv_ref)

    ba = ba_ref[...]
    beta_all = jax.nn.sigmoid(ba)
    g_all = -jnp.exp(alog_ref[...]) * jax.nn.softplus(ba + dt_ref[...])
    row_in_chunk = lax.broadcasted_iota(jnp.int32, (rows, dh), 0) & (c - 1)
    gc_all = g_all
    shift = 1
    while shift < c:
        gc_all = gc_all + jnp.where(row_in_chunk >= shift, pltpu.roll(gc_all, shift, 0), 0.0)
        shift *= 2
    gt_s[...] = jnp.transpose(gc_all)
    lane = lax.broadcasted_iota(jnp.int32, (rows, dh), 1)

    ri = lax.broadcasted_iota(jnp.int32, (rows, rows), 0)
    ci = lax.broadcasted_iota(jnp.int32, (rows, rows), 1)
    same = (ri // c) == (ci // c)
    lower = same & (ci <= ri)
    strict = same & (ci < ri)
    eye = jnp.where(ri == ci, 1.0, 0.0)

    outs = []
    for hh in range(hq_blk):
        q = q_all[:, hh * dh:(hh + 1) * dh]
        k = k_all[:, hh * dh:(hh + 1) * dh]
        qn = q * lax.rsqrt(jnp.sum(q * q, axis=-1, keepdims=True) + 1e-6) * (dh ** -0.5)
        kn = k * lax.rsqrt(jnp.sum(k * k, axis=-1, keepdims=True) + 1e-6)
        knb = kn.astype(BF16)
        kk = lax.dot_general(knb, knb, _NT, preferred_element_type=F32)
        qk = lax.dot_general(qn.astype(BF16), knb, _NT, preferred_element_type=F32)
        for e in range(rep):
            sidx = hh * rep + e
            hv = (hp * hq_blk + hh) * rep + e
            beta = jnp.sum(jnp.where(lane == hv, beta_all, 0.0), axis=1, keepdims=True)
            gc = jnp.sum(jnp.where(lane == DN_V_HEADS + hv, gc_all, 0.0), axis=1, keepdims=True)
            gc_row = gt_s[pl.ds(DN_V_HEADS + hv, 1), :]
            decay = jnp.exp(jnp.where(lower, gc - gc_row, NEG_INF))
            a_mat = jnp.where(strict, kk * beta * decay, 0.0)
            m = -a_mat
            p = eye + m
            span = 2
            while span < c:
                m = _bdot(m, m)
                p = p + _bdot(p, m)
                span *= 2
            ve = v_all[:, sidx * dh:(sidx + 1) * dh]
            rhs = jnp.concatenate([ve * beta, kn * (beta * jnp.exp(gc))], axis=1)
            uw = _bdot(p, rhs).astype(BF16)
            attn = jnp.where(lower, qk * decay, 0.0).astype(BF16)
            auw = jnp.dot(attn, uw, preferred_element_type=F32)
            g_last = jnp.concatenate(
                [jnp.broadcast_to(gc[(ch + 1) * c - 1:(ch + 1) * c], (c, 1)) for ch in range(nchunk)], axis=0)
            k_dec = (kn * jnp.exp(g_last - gc)).astype(BF16)
            q_eff = qn * jnp.exp(gc) - auw[:, dh:]
            o_chunks = []
            for ch in range(nchunk):
                r0 = ch * c
                bn = lax.dot_general(k_dec[r0:r0 + c], uw[r0:r0 + c], _TN, preferred_element_type=F32)
                state = state_s[sidx]
                lhs = jnp.concatenate([q_eff[r0:r0 + c], bn[:, dh:]], axis=0)
                prod = _bdot(lhs, state)
                o_chunks.append(prod[:c] + auw[r0:r0 + c, :dh])
                state_s[sidx] = state * jnp.exp(gc[r0 + c - 1:r0 + c]) - prod[c:] + bn[:, :dh]
            o = jnp.concatenate(o_chunks, axis=0)
            z = z_ref[:, sidx * dh:(sidx + 1) * dh]
            o = o * lax.rsqrt(jnp.mean(o * o, axis=-1, keepdims=True) + NORM_EPS) * on_ref[...] * (z * jax.nn.sigmoid(z))
            outs.append(o)
    o_ref[...] = jnp.concatenate(outs, axis=1).astype(o_ref.dtype)


def _dn_mix(y, bsz, seqlen, conv_w, a_log, dt_bias, o_norm):
    dh, hq_n, hv_n = DN_HEAD_DIM, DN_QK_HEADS, DN_V_HEADS
    rep = hv_n // hq_n
    rows, hq_blk = DN_ROW_BLOCK, DN_HEAD_BLOCK
    assert seqlen % rows == 0 and 2 * hv_n <= dh and hq_n % hq_blk == 0
    nb = seqlen // rows
    qk_dim, v_dim = hq_n * dh, hv_n * dh
    qb, vb = hq_blk * dh, hq_blk * rep * dh
    pad = jnp.zeros((dh - 2 * hv_n,), F32)
    alog_row = jnp.concatenate([jnp.zeros((hv_n,), F32), a_log, pad]).reshape(1, dh)
    dt_row = jnp.concatenate([jnp.zeros((hv_n,), F32), dt_bias, pad]).reshape(1, dh)
    taps = conv_w.shape[0]
    return pl.pallas_call(
        functools.partial(_dn_body, rows=rows, hq_blk=hq_blk),
        grid=(bsz, hq_n // hq_blk, nb),
        in_specs=[
            pl.BlockSpec((rows, qb), lambda b, h, i: (b * nb + i, h)),
            pl.BlockSpec((rows, qb), lambda b, h, i: (b * nb + i, qk_dim // qb + h)),
            pl.BlockSpec((rows, vb), lambda b, h, i: (b * nb + i, 2 * qk_dim // vb + h)),
            pl.BlockSpec((rows, vb), lambda b, h, i: (b * nb + i, (2 * qk_dim + v_dim) // vb + h)),
            pl.BlockSpec((rows, dh), lambda b, h, i: (b * nb + i, (2 * qk_dim + 2 * v_dim) // dh)),
            pl.BlockSpec((taps, qb), lambda b, h, i: (0, h)),
            pl.BlockSpec((taps, qb), lambda b, h, i: (0, qk_dim // qb + h)),
            pl.BlockSpec((taps, vb), lambda b, h, i: (0, 2 * qk_dim // vb + h)),
            pl.BlockSpec((1, dh), lambda b, h, i: (0, 0)),
            pl.BlockSpec((1, dh), lambda b, h, i: (0, 0)),
            pl.BlockSpec((1, dh), lambda b, h, i: (0, 0)),
        ],
        out_specs=pl.BlockSpec((rows, vb), lambda b, h, i: (b * nb + i, h)),
        out_shape=jax.ShapeDtypeStruct((bsz * seqlen, v_dim), BF16),
        scratch_shapes=[
            pltpu.VMEM((DN_CONV_HALO + rows, qb), F32), pltpu.VMEM((DN_CONV_HALO + rows, qb), F32),
            pltpu.VMEM((DN_CONV_HALO + rows, vb), F32), pltpu.VMEM((hq_blk * rep, dh, dh), F32),
            pltpu.VMEM((dh, rows), F32),
        ],
        name="dn_chunk_rule",
        compiler_params=_cparams("parallel", "parallel", "arbitrary"),
    )(y, y, y, y, y, conv_w, conv_w, conv_w, alog_row, dt_row, o_norm.reshape(1, dh))


def _gated_deltanet_res(h, u, bsz, seqlen, j, w_in, conv_w, a_log, dt_bias, o_norm, w_out):
    y = _mm(u, w_in, j, tm=2048, tn=512)
    o = _dn_mix(y, bsz, seqlen, conv_w, a_log, dt_bias, o_norm)
    return _mm(o, w_out, j, res=h, tm=1024, tn=512)


CV_ROW_BLOCK = 256
CV_CONV_HALO = 32
CV_LANE_CHUNK = 512


def _cv_body(val_ref, gate_ref, dw_ref, dwb_ref, g_ref, b_ref, o_ref, ext_s, conv_s, *, rows):
    width, ch = dw_ref.shape
    halo = CV_CONV_HALO

    @pl.when(pl.program_id(1) == 0)
    def _():
        ext_s[:halo, :] = jnp.zeros((halo, ch), F32)

    gate = gate_ref[...]
    ext_s[halo:, :] = val_ref[...] * jax.nn.sigmoid(gate)
    for c0 in range(0, ch, CV_LANE_CHUNK):
        cs = slice(c0, c0 + CV_LANE_CHUNK)
        acc = jnp.zeros((rows, CV_LANE_CHUNK), F32) + dwb_ref[:, cs]
        for k in range(width):
            off = halo - (width - 1) + k
            acc = acc + dw_ref[k:k + 1, cs] * ext_s[off:off + rows, cs]
        conv_s[:, cs] = acc
    ext_s[:halo, :] = ext_s[rows:rows + halo, :]
    u = conv_s[...]
    mu = jnp.mean(u, axis=-1, keepdims=True)
    d = u - mu
    var = jnp.mean(d * d, axis=-1, keepdims=True)
    u = d * lax.rsqrt(var + NORM_EPS) * g_ref[...] + b_ref[...]
    o_ref[...] = (u * jax.nn.sigmoid(u)).astype(o_ref.dtype)


def _cv_mix(y, bsz, seqlen, dw, dw_b, ln_g, ln_b):
    rows = CV_ROW_BLOCK
    width, ch = dw.shape
    assert seqlen % rows == 0 and width - 1 <= CV_CONV_HALO <= rows and ch % CV_LANE_CHUNK == 0
    nb = seqlen // rows
    vec = lambda v: v.reshape(1, ch)
    vspec = pl.BlockSpec((1, ch), lambda b, i: (0, 0))
    return pl.pallas_call(
        functools.partial(_cv_body, rows=rows),
        grid=(bsz, nb),
        in_specs=[
            pl.BlockSpec((rows, ch), lambda b, i: (b * nb + i, 0)),
            pl.BlockSpec((rows, ch), lambda b, i: (b * nb + i, 1)),
            pl.BlockSpec((width, ch), lambda b, i: (0, 0)),
            vspec, vspec, vspec,
        ],
        out_specs=pl.BlockSpec((rows, ch), lambda b, i: (b * nb + i, 0)),
        out_shape=jax.ShapeDtypeStruct((bsz * seqlen, ch), BF16),
        scratch_shapes=[pltpu.VMEM((CV_CONV_HALO + rows, ch), F32), pltpu.VMEM((rows, ch), F32)],
        name="conformer_conv",
        compiler_params=_cparams("parallel", "arbitrary"),
    )(y, y, dw, vec(dw_b), vec(ln_g), vec(ln_b))


def _conformer_res(h, u, bsz, seqlen, j, w_in, b_in, dw, dw_b, ln_g, ln_b, w_out, b_out):
    y = _mm(u, w_in, j, bias=b_in, tm=2048, tn=512)
    a = _cv_mix(y, bsz, seqlen, dw, dw_b, ln_g, ln_b)
    return _mm(a, w_out, j, bias=b_out, res=h, tm=2048, tn=512)


NSA_Q_TILE = 256
NSA_KEY_TILE = 512
NSA_MASK_BIAS = 2.0 ** 60


def _nsa_compress_body(x_ref, w1_ref, w2_ref, o_ref):
    hid = jnp.dot(x_ref[...], w1_ref[...].astype(BF16), preferred_element_type=F32)
    hid = hid * jax.nn.sigmoid(hid)
    o_ref[...] = jnp.dot(hid.astype(BF16), w2_ref[...].astype(BF16), preferred_element_type=F32).astype(o_ref.dtype)


def _nsa_compress(blk, w1, w2, tr):
    two, rows, kdim = blk.shape
    hdim, dh = w2.shape[1], w2.shape[2]
    return pl.pallas_call(
        _nsa_compress_body,
        grid=(two, rows // tr),
        in_specs=[pl.BlockSpec((None, tr, kdim), lambda s, i: (s, i, 0)),
                  pl.BlockSpec((None, kdim, hdim), lambda s, i: (s, 0, 0)),
                  pl.BlockSpec((None, hdim, dh), lambda s, i: (s, 0, 0))],
        out_specs=pl.BlockSpec((None, tr, dh), lambda s, i: (s, i, 0)),
        out_shape=jax.ShapeDtypeStruct((two, rows, dh), BF16),
        name="nsa_compress",
        compiler_params=_cparams("parallel", "arbitrary"),
    )(blk, w1, w2)


def _softmax_rows(s):
    m = jnp.max(s, axis=-1, keepdims=True)
    e = jnp.exp(s - m)
    return e / jnp.sum(e, axis=-1, keepdims=True)


def _nsa_attn_body(q_ref, ks_ref, vs_ref, kw_ref, vw_ref, gl_ref, kc_ref, vc_ref, cov_ref, o_ref,
                   kaug_s, vs_s, kw_s, vw_s, m_s, l_s, acc_s, *, tq, tk, seqlen):
    qi = pl.program_id(2)
    nr = NSA_HEADS // NSA_KV_GROUPS
    dh = NSA_HEAD_DIM
    rows = nr * tq
    nsel = seqlen // NSA_SEL_BLOCK
    ncmp = seqlen // NSA_CMP_STRIDE

    @pl.when(qi == 0)
    def _():
        row_blk = lax.broadcasted_iota(jnp.int32, (seqlen, dh), 0) // NSA_SEL_BLOCK
        lane = lax.broadcasted_iota(jnp.int32, (seqlen, dh), 1)
        kaug_s[:, :dh] = ks_ref[...].astype(BF16)
        kaug_s[:, dh:] = jnp.where(lane == row_blk, 1.0, 0.0).astype(BF16)
        vs_s[...] = vs_ref[...].astype(BF16)
        kw_s[...] = kw_ref[...].astype(BF16)
        vw_s[...] = vw_ref[...].astype(BF16)

    q = q_ref[...]
    qs = jnp.concatenate([q[:, r * dh:(r + 1) * dh] for r in range(nr)], axis=0) * (dh ** -0.5)
    qb = qs.astype(BF16)
    pos = qi * tq + (lax.broadcasted_iota(jnp.int32, (rows, 1), 0) & (tq - 1))

    s = lax.dot_general(qb, kc_ref[...], _NT, preferred_element_type=F32)
    blk_end = lax.broadcasted_iota(jnp.int32, (rows, ncmp), 1) * NSA_CMP_STRIDE + (NSA_CMP_BLOCK - 1)
    valid = blk_end <= pos
    p = jnp.where(valid, _softmax_rows(jnp.where(valid, s, NEG_INF)), 0.0)
    o_cmp = jnp.dot(p.astype(BF16), vc_ref[...], preferred_element_type=F32)
    psum = p[0:tq] + p[tq:2 * tq] + p[2 * tq:3 * tq] + p[3 * tq:4 * tq]
    p_hi = psum.astype(BF16)
    p_lo = (psum - p_hi.astype(F32)).astype(BF16)
    cov = cov_ref[...]
    imp_t = (lax.dot_general(cov, p_hi, _NT, preferred_element_type=F32)
             + lax.dot_general(cov, p_lo, _NT, preferred_element_type=F32))

    jb = lax.broadcasted_iota(jnp.int32, (nsel, tq), 0)
    post = qi * tq + lax.broadcasted_iota(jnp.int32, (nsel, tq), 1)
    cur = post // NSA_SEL_BLOCK
    forced = (jb == 0) | (jb == cur) | (jb == cur - 1)
    causal = jb * NSA_SEL_BLOCK <= post
    score = jnp.where(forced, NSA_FORCE, jnp.where(causal, imp_t, -NSA_FORCE))
    rank = jnp.zeros((nsel, tq), F32)
    for k in range(nsel):
        rk = score[k:k + 1, :]
        rank = rank + jnp.where(rk > score, 1.0, jnp.where((rk == score) & (jb > k), 1.0, 0.0))
    n_top = min(NSA_TOP_N, nsel)
    bias_t = jnp.where(rank < n_top, 0.0, -NSA_MASK_BIAS)
    bias_t = jnp.concatenate([bias_t, jnp.zeros((dh - nsel, tq), F32)], axis=0)
    bias = jnp.transpose(bias_t).astype(BF16)
    qaug = jnp.concatenate([qb, jnp.concatenate([bias] * nr, axis=0)], axis=1)

    m_s[...] = jnp.full(m_s.shape, NEG_INF, F32)
    l_s[...] = jnp.zeros(l_s.shape, F32)
    acc_s[...] = jnp.zeros(acc_s.shape, F32)

    def sel_step(kt, carry, causal_mask):
        k0 = pl.multiple_of(kt * tk, tk)
        st = lax.dot_general(qaug, kaug_s[pl.ds(k0, tk), :], _NT, preferred_element_type=F32)
        if causal_mask:
            kpos = k0 + lax.broadcasted_iota(jnp.int32, (rows, tk), 1)
            st = jnp.where(kpos <= pos, st, NEG_INF)
        m_prev = m_s[...]
        m_new = jnp.maximum(m_prev, jnp.max(st, axis=-1, keepdims=True))
        alpha = jnp.exp(m_prev - m_new)
        pt = jnp.exp(st - m_new)
        l_s[...] = alpha * l_s[...] + jnp.sum(pt, axis=-1, keepdims=True)
        acc_s[...] = alpha * acc_s[...] + jnp.dot(pt.astype(BF16), vs_s[pl.ds(k0, tk), :],
                                                  preferred_element_type=F32)
        m_s[...] = m_new
        return carry

    n_full = (qi * tq) // tk
    lax.fori_loop(0, n_full, functools.partial(sel_step, causal_mask=False), 0)
    sel_step(n_full, 0, causal_mask=True)
    o_sel = acc_s[...] / l_s[...]

    wk = NSA_WINDOW + tq
    w0 = pl.multiple_of(jnp.maximum(qi * tq - NSA_WINDOW, 0), tq)
    sw = lax.dot_general(qb, kw_s[pl.ds(w0, wk), :], _NT, preferred_element_type=F32)
    diff = pos - (w0 + lax.broadcasted_iota(jnp.int32, (rows, wk), 1))
    pw = _softmax_rows(jnp.where((diff >= 0) & (diff < NSA_WINDOW), sw, NEG_INF))
    o_win = jnp.dot(pw.astype(BF16), vw_s[pl.ds(w0, wk), :], preferred_element_type=F32)

    gate = jax.nn.sigmoid(gl_ref[...])
    outs = []
    for r in range(nr):
        sl = slice(r * tq, (r + 1) * tq)
        outs.append(gate[:, 3 * r:3 * r + 1] * o_cmp[sl] + gate[:, 3 * r + 1:3 * r + 2] * o_sel[sl]
                    + gate[:, 3 * r + 2:3 * r + 3] * o_win[sl])
    o_ref[...] = jnp.concatenate(outs, axis=1).astype(o_ref.dtype)


def _nsa_cover(seqlen):
    n_sel = seqlen // NSA_SEL_BLOCK
    ncmp = seqlen // NSA_CMP_STRIDE
    per = NSA_SEL_BLOCK // NSA_CMP_STRIDE
    cover = np.zeros((n_sel, ncmp), np.float32)
    for j in range(n_sel):
        for m in range(per):
            for n in range(NSA_CMP_BLOCK // NSA_CMP_STRIDE):
                c = per * j + m - n
                if 0 <= c < ncmp - 1:
                    cover[j, c] += 1.0
    return cover


def _nsa_mix(y, bsz, seqlen, pe_k, pe_v, ck_w1, ck_w2, cv_w1, cv_w2):
    ng, nr, dh = NSA_KV_GROUPS, NSA_HEADS // NSA_KV_GROUPS, NSA_HEAD_DIM
    q_dim, kv_dim = NSA_HEADS * dh, ng * dh
    tq, tk = NSA_Q_TILE, NSA_KEY_TILE
    assert seqlen % tk == 0 and tk % tq == 0 and seqlen >= NSA_WINDOW + tq and seqlen // NSA_SEL_BLOCK <= dh
    ns = seqlen // NSA_CMP_STRIDE
    y3 = y.reshape(bsz, seqlen, -1)

    def blocks(col0, pe):
        t = y3[:, :, col0:col0 + kv_dim].reshape(bsz, ns, NSA_CMP_STRIDE, ng, dh)
        blk = jnp.concatenate([t[:, 0:ns - 1], t[:, 1:ns]], axis=2) + pe[None, None, :, None, :]
        blk = jnp.moveaxis(blk, 3, 1).reshape(bsz, ng, ns - 1, NSA_CMP_BLOCK * dh)
        blk = jnp.pad(blk, ((0, 0), (0, 0), (0, 1), (0, 0)))
        return blk.reshape(bsz * ng * ns, NSA_CMP_BLOCK * dh).astype(BF16)

    blk = jnp.stack([blocks(q_dim, pe_k), blocks(q_dim + kv_dim, pe_v)])
    cmp = _nsa_compress(blk, jnp.stack([ck_w1, cv_w1]), jnp.stack([ck_w2, cv_w2]), ns)
    cmp = cmp.reshape(2, bsz * ng, ns, dh)
    gl = y3[:, :, q_dim + 6 * kv_dim:].reshape(bsz * seqlen, ng, nr * 3)
    gl = jnp.pad(jnp.moveaxis(gl, 1, 0), ((0, 0), (0, 0), (0, dh - nr * 3)))
    cover = jnp.asarray(_nsa_cover(seqlen), BF16)
    nq = seqlen // tq
    kvc = kv_dim // dh
    c0 = q_dim // dh

    def kv_spec(section):
        return pl.BlockSpec((seqlen, dh), lambda b, g, i, s=section: (b, c0 + s * kvc + g))

    return pl.pallas_call(
        functools.partial(_nsa_attn_body, tq=tq, tk=tk, seqlen=seqlen),
        grid=(bsz, ng, nq),
        in_specs=[
            pl.BlockSpec((tq, nr * dh), lambda b, g, i: (b * nq + i, g)),
            kv_spec(2), kv_spec(3), kv_spec(4), kv_spec(5),
            pl.BlockSpec((None, tq, dh), lambda b, g, i: (g, b * nq + i, 0)),
            pl.BlockSpec((None, None, ns, dh), lambda b, g, i: (0, b * ng + g, 0, 0)),
            pl.BlockSpec((None, None, ns, dh), lambda b, g, i: (1, b * ng + g, 0, 0)),
            pl.BlockSpec((seqlen // NSA_SEL_BLOCK, ns), lambda b, g, i: (0, 0)),
        ],
        out_specs=pl.BlockSpec((tq, nr * dh), lambda b, g, i: (b * nq + i, g)),
        out_shape=jax.ShapeDtypeStruct((bsz * seqlen, q_dim), BF16),
        scratch_shapes=[
            pltpu.VMEM((seqlen, 2 * dh), BF16), pltpu.VMEM((seqlen, dh), BF16),
            pltpu.VMEM((seqlen, dh), BF16), pltpu.VMEM((seqlen, dh), BF16),
            pltpu.VMEM((nr * tq, 1), F32), pltpu.VMEM((nr * tq, 1), F32), pltpu.VMEM((nr * tq, dh), F32),
        ],
        name="nsa_attention",
        compiler_params=_cparams("parallel", "parallel", "arbitrary"),
    )(y, y, y, y, y, gl, cmp, cmp, cover)


def _nsa_res(h, u, bsz, seqlen, j, w_in, pe_k, pe_v, ck_w1, ck_w2, cv_w1, cv_w2, w_out):
    y = _mm(u, w_in, j, tm=2048, tn=256)
    o = _nsa_mix(y, bsz, seqlen, pe_k, pe_v, ck_w1, ck_w2, cv_w1, cv_w2)
    return _mm(o, w_out, j, res=h, tm=2048, tn=512)


def kernel(x, mix_norm, ffn_norm, final_norm, dn_w_in, dn_conv, dn_a_log, dn_dt_bias, dn_o_norm, dn_w_out, cv_w_in, cv_b_in, cv_dw, cv_dw_b, cv_ln_g, cv_ln_b, cv_w_out, cv_b_out, nsa_w_in, nsa_pe_k, nsa_pe_v, nsa_ck_w1, nsa_ck_w2, nsa_cv_w1, nsa_cv_w2, nsa_w_out, ffn_w1, ffn_w3, ffn_w2, moe_router, moe_w1, moe_w3, moe_w2):
    bsz, seqlen, d = x.shape
    depth = mix_norm.shape[0]
    h = x.reshape(bsz * seqlen, d)
    counts = [0, 0, 0]
    for i in range(depth):
        u = _rms_norm(h, mix_norm[i])
        kind = i % 3
        j = counts[kind]
        counts[kind] += 1
        if kind == 0:
            h = _gated_deltanet_res(h, u, bsz, seqlen, j, dn_w_in, dn_conv[j], dn_a_log[j], dn_dt_bias[j],
                                    dn_o_norm[j], dn_w_out)
        elif kind == 1:
            h = _conformer_res(h, u, bsz, seqlen, j, cv_w_in, cv_b_in[j], cv_dw[j], cv_dw_b[j], cv_ln_g[j],
                               cv_ln_b[j], cv_w_out, cv_b_out[j])
        else:
            h = _nsa_res(h, u, bsz, seqlen, j, nsa_w_in, nsa_pe_k[j], nsa_pe_v[j], nsa_ck_w1[j], nsa_ck_w2[j],
                         nsa_cv_w1[j], nsa_cv_w2[j], nsa_w_out)
        u = _rms_norm(h, ffn_norm[i])
        if i % 2 == 0:
            h = _swiglu_res(h, u, ffn_w1, ffn_w3, ffn_w2, i // 2)
        else:
            h = _moe_res(h, u, moe_router[i // 2], moe_w1, moe_w3, moe_w2, i // 2)
    return _rms_norm(h, final_norm).reshape(bsz, seqlen, d)
```

```python
import functools

import numpy as np
import jax
import jax.numpy as jnp
from jax import lax
from jax.experimental import pallas as pl
from jax.experimental.pallas import tpu as pltpu

F32 = jnp.float32
BF16 = jnp.bfloat16

V7X_VMEM_LIMIT_BYTES = 56 * 1024 * 1024

NORM_EPS = 1e-6
NEG_INF = -1e30

DN_QK_HEADS = 16
DN_V_HEADS = 32
DN_HEAD_DIM = 128
DN_CHUNK = 64

NSA_HEADS = 16
NSA_KV_GROUPS = 4
NSA_HEAD_DIM = 128
NSA_CMP_BLOCK = 32
NSA_CMP_STRIDE = 16
NSA_SEL_BLOCK = 64
NSA_TOP_N = 16
NSA_WINDOW = 512
NSA_FORCE = 1e9

N_EXPERTS = 8
TOP_K = 2
MOE_ROW_TILE = 512

_NT = (((1,), (1,)), ((), ()))
_TN = (((0,), (0,)), ((), ()))


def _cparams(*sem):
    return pltpu.CompilerParams(dimension_semantics=sem, vmem_limit_bytes=V7X_VMEM_LIMIT_BYTES)


MM_SUB_TILE = 512


def _sub_tiles(shape):
    tm, tn = shape
    sm, sn = min(tm, MM_SUB_TILE), min(tn, MM_SUB_TILE)
    return [(slice(r, r + sm), slice(c, c + sn)) for r in range(0, tm, sm) for c in range(0, tn, sn)]


def _weight_changed(te_ref, i):
    prev = te_ref[jnp.maximum(i - 1, 0)]
    return jnp.logical_or(i == 0, te_ref[i] != prev)


def _gmm_body(te_ref, nu_ref, x_ref, w_ref, *rest, has_bias, has_res):
    rest = list(rest)
    b_ref = rest.pop(0) if has_bias else None
    r_ref = rest.pop(0) if has_res else None
    o_ref, wb_ref = rest
    i = pl.program_id(1)

    @pl.when(_weight_changed(te_ref, i))
    def _():
        wb_ref[...] = w_ref[...].astype(BF16)

    @pl.when(i < nu_ref[0])
    def _():
        for rs, cs in _sub_tiles(o_ref.shape):
            acc = jnp.dot(x_ref[rs, :], wb_ref[:, cs], preferred_element_type=F32)
            if has_bias:
                acc = acc + b_ref[:, cs]
            if has_res:
                acc = acc + r_ref[rs, cs]
            o_ref[rs, cs] = acc.astype(o_ref.dtype)

    @pl.when(i >= nu_ref[0])
    def _():
        o_ref[...] = jnp.zeros(o_ref.shape, o_ref.dtype)


def _gmm(x, w, te, nu, *, bias=None, res=None, out_dtype=F32, tm, tn):
    m, k = x.shape
    n = w.shape[2]
    assert m % tm == 0 and w.shape[1] == k
    grid = (pl.cdiv(n, tn), m // tm)
    in_specs = [
        pl.BlockSpec((tm, k), lambda j, i, te, nu: (i, 0)),
        pl.BlockSpec((None, k, tn), lambda j, i, te, nu: (te[i], 0, j)),
    ]
    args = [x, w]
    if bias is not None:
        in_specs.append(pl.BlockSpec((1, tn), lambda j, i, te, nu: (0, j)))
        args.append(bias.reshape(1, n).astype(F32))
    if res is not None:
        in_specs.append(pl.BlockSpec((tm, tn), lambda j, i, te, nu: (i, j)))
        args.append(res)
    return pl.pallas_call(
        functools.partial(_gmm_body, has_bias=bias is not None, has_res=res is not None),
        grid_spec=pltpu.PrefetchScalarGridSpec(
            num_scalar_prefetch=2,
            grid=grid,
            in_specs=in_specs,
            out_specs=pl.BlockSpec((tm, tn), lambda j, i, te, nu: (i, j)),
            scratch_shapes=[pltpu.VMEM((k, tn), BF16)],
        ),
        out_shape=jax.ShapeDtypeStruct((m, n), out_dtype),
        name="grouped_matmul",
        compiler_params=_cparams("parallel", "arbitrary"),
    )(te, nu, *args)


def _gglu_body(te_ref, nu_ref, x_ref, w1_ref, w3_ref, o_ref, w1b_ref, w3b_ref):
    i = pl.program_id(1)

    @pl.when(_weight_changed(te_ref, i))
    def _():
        w1b_ref[...] = w1_ref[...].astype(BF16)
        w3b_ref[...] = w3_ref[...].astype(BF16)

    @pl.when(i < nu_ref[0])
    def _():
        for rs, cs in _sub_tiles(o_ref.shape):
            x = x_ref[rs, :]
            a = jnp.dot(x, w1b_ref[:, cs], preferred_element_type=F32)
            b = jnp.dot(x, w3b_ref[:, cs], preferred_element_type=F32)
            o_ref[rs, cs] = (a * jax.nn.sigmoid(a) * b).astype(o_ref.dtype)

    @pl.when(i >= nu_ref[0])
    def _():
        o_ref[...] = jnp.zeros(o_ref.shape, o_ref.dtype)


def _gglu(x, w1, w3, te, nu, *, tm, tn):
    m, k = x.shape
    n = w1.shape[2]
    assert m % tm == 0
    grid = (pl.cdiv(n, tn), m // tm)
    wspec = pl.BlockSpec((None, k, tn), lambda j, i, te, nu: (te[i], 0, j))
    return pl.pallas_call(
        _gglu_body,
        grid_spec=pltpu.PrefetchScalarGridSpec(
            num_scalar_prefetch=2,
            grid=grid,
            in_specs=[pl.BlockSpec((tm, k), lambda j, i, te, nu: (i, 0)), wspec, wspec],
            out_specs=pl.BlockSpec((tm, tn), lambda j, i, te, nu: (i, j)),
            scratch_shapes=[pltpu.VMEM((k, tn), BF16), pltpu.VMEM((k, tn), BF16)],
        ),
        out_shape=jax.ShapeDtypeStruct((m, n), BF16),
        name="grouped_swiglu_up",
        compiler_params=_cparams("parallel", "arbitrary"),
    )(te, nu, x, w1, w3)


def _dense_tiles(m, tm, slab):
    nt = m // tm
    return jnp.full((nt,), slab, jnp.int32), jnp.full((1,), nt, jnp.int32)


def _mm(x, w, slab, *, bias=None, res=None, out_dtype=F32, tm=512, tn=512):
    te, nu = _dense_tiles(x.shape[0], tm, slab)
    return _gmm(x.astype(BF16), w, te, nu, bias=bias, res=res, out_dtype=out_dtype, tm=tm, tn=tn)


def _rms_norm(x, g):
    return x * lax.rsqrt(jnp.mean(x * x, axis=-1, keepdims=True) + NORM_EPS) * g


def _swiglu_res(h, u, w1, w3, w2, layer):
    ub = u.astype(BF16)
    te, nu = _dense_tiles(ub.shape[0], 1024, layer)
    act = _gglu(ub, w1, w3, te, nu, tm=1024, tn=512)
    te, nu = _dense_tiles(ub.shape[0], 512, layer)
    return _gmm(act, w2, te, nu, res=h, tm=512, tn=512)


MOE_COMBINE_TILE = 256
MOE_GATE_LANES = 128


def _row_gather_pipeline(idx_ref, src_hbm, buf, sem, n_rows, slot_of):
    i = pl.program_id(0)

    def row_copy(half, r, src_row):
        return pltpu.make_async_copy(src_hbm.at[pl.ds(src_row, 1), :], buf.at[half, pl.ds(slot_of(r), 1), :],
                                     sem.at[half])

    def start_tile(tile):
        def issue(r, carry):
            row_copy(tile % 2, r, idx_ref[tile * n_rows + r]).start()
            return carry
        lax.fori_loop(0, n_rows, issue, 0, unroll=8)

    @pl.when(i == 0)
    def _():
        start_tile(0)

    @pl.when(i + 1 < pl.num_programs(0))
    def _():
        start_tile(i + 1)

    def drain(r, carry):
        row_copy(i % 2, r, 0).wait()
        return carry

    lax.fori_loop(0, n_rows, drain, 0, unroll=8)
    return i % 2


def _moe_combine_body(pos_ref, h_ref, w_ref, ys_hbm, o_ref, buf, sem, *, tt):
    half = _row_gather_pipeline(pos_ref, ys_hbm, buf, sem, tt * TOP_K,
                                lambda r: (r % TOP_K) * tt + r // TOP_K)
    acc = h_ref[...]
    w = w_ref[...]
    for k in range(TOP_K):
        acc = acc + w[:, k:k + 1] * buf[half, k * tt:(k + 1) * tt, :]
    o_ref[...] = acc


def _moe_combine(h, ys, pos, top_w):
    m, d = h.shape
    tt = MOE_COMBINE_TILE
    assert m % tt == 0
    w = jnp.pad(top_w.astype(F32), ((0, 0), (0, MOE_GATE_LANES - TOP_K)))
    return pl.pallas_call(
        functools.partial(_moe_combine_body, tt=tt),
        grid_spec=pltpu.PrefetchScalarGridSpec(
            num_scalar_prefetch=1,
            grid=(m // tt,),
            in_specs=[pl.BlockSpec((tt, d), lambda i, pos: (i, 0)),
                      pl.BlockSpec((tt, MOE_GATE_LANES), lambda i, pos: (i, 0)),
                      pl.BlockSpec(memory_space=pl.ANY)],
            out_specs=pl.BlockSpec((tt, d), lambda i, pos: (i, 0)),
            scratch_shapes=[pltpu.VMEM((2, TOP_K * tt, d), F32), pltpu.SemaphoreType.DMA((2,))],
        ),
        out_shape=jax.ShapeDtypeStruct((m, d), F32),
        name="moe_combine",
        compiler_params=_cparams("arbitrary"),
    )(pos, h, w, ys)


def _moe_dispatch_body(tok_ref, u_hbm, o_ref, buf, sem, *, tr):
    half = _row_gather_pipeline(tok_ref, u_hbm, buf, sem, tr, lambda r: r)
    o_ref[...] = buf[half].astype(o_ref.dtype)


def _moe_dispatch(u, row_token):
    d = u.shape[1]
    rows = row_token.shape[0]
    tr = MOE_COMBINE_TILE
    assert rows % tr == 0
    return pl.pallas_call(
        functools.partial(_moe_dispatch_body, tr=tr),
        grid_spec=pltpu.PrefetchScalarGridSpec(
            num_scalar_prefetch=1,
            grid=(rows // tr,),
            in_specs=[pl.BlockSpec(memory_space=pl.ANY)],
            out_specs=pl.BlockSpec((tr, d), lambda i, tok: (i, 0)),
            scratch_shapes=[pltpu.VMEM((2, tr, d), F32), pltpu.SemaphoreType.DMA((2,))],
        ),
        out_shape=jax.ShapeDtypeStruct((rows, d), BF16),
        name="moe_dispatch",
        compiler_params=_cparams("arbitrary"),
    )(row_token, u)


def _moe_res(h, u, router, w1, w3, w2, layer):
    m, d = u.shape
    ts = MOE_ROW_TILE
    w1, w3, w2 = (w.reshape((-1,) + w.shape[2:]) for w in (w1, w3, w2))
    logits = jnp.dot(u, router, precision=lax.Precision.HIGHEST)
    top_val, top_idx = lax.top_k(logits, TOP_K)
    top_w = jax.nn.softmax(top_val, axis=-1)
    flat_e = top_idx.reshape(-1).astype(jnp.int32)
    order = jnp.argsort(flat_e, stable=True).astype(jnp.int32)
    sizes = jnp.zeros((N_EXPERTS,), jnp.int32).at[flat_e].add(1)
    starts = jnp.cumsum(sizes) - sizes
    psizes = ((sizes + ts - 1) // ts) * ts
    pends = jnp.cumsum(psizes)
    pstarts = pends - psizes
    sorted_e = flat_e[order]
    dest = pstarts[sorted_e] + (jnp.arange(m * TOP_K, dtype=jnp.int32) - starts[sorted_e])
    mp = m * TOP_K + N_EXPERTS * ts
    row_token = jnp.zeros((mp,), jnp.int32).at[dest].set(order // TOP_K)
    pos = jnp.zeros((m * TOP_K,), jnp.int32).at[order].set(dest)
    nt = mp // ts
    te = jnp.minimum(jnp.searchsorted(pends, jnp.arange(nt, dtype=jnp.int32) * ts, side="right"),
                     N_EXPERTS - 1).astype(jnp.int32) + layer * N_EXPERTS
    nu = (pends[-1] // ts).astype(jnp.int32).reshape(1)
    xs = _moe_dispatch(u, row_token)
    act = _gglu(xs, w1, w3, te, nu, tm=ts, tn=1024)
    ys = _gmm(act, w2, te, nu, tm=ts, tn=512)
    return _moe_combine(h, ys, pos, top_w)


DN_ROW_BLOCK = 256
DN_HEAD_BLOCK = 4
DN_CONV_HALO = 8


def _bdot(a, b):
    return jnp.dot(a.astype(BF16), b.astype(BF16), preferred_element_type=F32)


def _dn_conv_silu(x_ref, ext_ref, w_ref):
    rows = x_ref.shape[0]
    width = w_ref.shape[0]
    w = w_ref[...]
    ext_ref[DN_CONV_HALO:, :] = x_ref[...]
    acc = w[width - 1:width, :] * x_ref[...]
    for k in range(width - 1):
        off = DN_CONV_HALO - (width - 1) + k
        acc = acc + w[k:k + 1, :] * ext_ref[off:off + rows, :]
    ext_ref[:DN_CONV_HALO, :] = x_ref[rows - DN_CONV_HALO:, :]
    return acc * jax.nn.sigmoid(acc)


def _dn_body(q_ref, k_ref, v_ref, z_ref, ba_ref, wq_ref, wk_ref, wv_ref, alog_ref, dt_ref, on_ref, o_ref,
             tq_s, tk_s, tv_s, state_s, gt_s, *, rows, hq_blk):
    hp = pl.program_id(1)
    dh = DN_HEAD_DIM
    c = DN_CHUNK
    nchunk = rows // c
    rep = DN_V_HEADS // DN_QK_HEADS

    @pl.when(pl.program_id(2) == 0)
    def _():
        tq_s[:DN_CONV_HALO, :] = jnp.zeros((DN_CONV_HALO, tq_s.shape[1]), F32)
        tk_s[:DN_CONV_HALO, :] = jnp.zeros((DN_CONV_HALO, tk_s.shape[1]), F32)
        tv_s[:DN_CONV_HALO, :] = jnp.zeros((DN_CONV_HALO, tv_s.shape[1]), F32)
        state_s[...] = jnp.zeros(state_s.shape, F32)

    q_all = _dn_conv_silu(q_ref, tq_s, wq_ref)
    k_all = _dn_conv_silu(k_ref, tk_s, wk_ref)
    v_all = _dn_conv_silu(v_ref, tv_s, wv_ref)

    ba = ba_ref[...]
    beta_all = jax.nn.sigmoid(ba)
    g_all = -jnp.exp(alog_ref[...]) * jax.nn.softplus(ba + dt_ref[...])
    row_in_chunk = lax.broadcasted_iota(jnp.int32, (rows, dh), 0) & (c - 1)
    gc_all = g_all
    shift = 1
    while shift < c:
        gc_all = gc_all + jnp.where(row_in_chunk >= shift, pltpu.roll(gc_all, shift, 0), 0.0)
        shift *= 2
    gt_s[...] = jnp.transpose(gc_all)
    lane = lax.broadcasted_iota(jnp.int32, (rows, dh), 1)

    ri = lax.broadcasted_iota(jnp.int32, (rows, rows), 0)
    ci = lax.broadcasted_iota(jnp.int32, (rows, rows), 1)
    same = (ri // c) == (ci // c)
    lower = same & (ci <= ri)
    strict = same & (ci < ri)

    outs = []
    for hh in range(hq_blk):
        q = q_all[:, hh * dh:(hh + 1) * dh]
        k = k_all[:, hh * dh:(hh + 1) * dh]
        qn = q * lax.rsqrt(jnp.sum(q * q, axis=-1, keepdims=True) + 1e-6) * (dh ** -0.5)
        kn = k * lax.rsqrt(jnp.sum(k * k, axis=-1, keepdims=True) + 1e-6)
        knb = kn.astype(BF16)
        kk = lax.dot_general(knb, knb, _NT, preferred_element_type=F32)
        qk = lax.dot_general(qn.astype(BF16), knb, _NT, preferred_element_type=F32)
        for e in range(rep):
            sidx = hh * rep + e
            hv = (hp * hq_blk + hh) * rep + e
            beta = jnp.sum(jnp.where(lane == hv, beta_all, 0.0), axis=1, keepdims=True)
            gc = jnp.sum(jnp.where(lane == DN_V_HEADS + hv, gc_all, 0.0), axis=1, keepdims=True)
            gc_row = gt_s[pl.ds(DN_V_HEADS + hv, 1), :]
            decay = jnp.exp(jnp.where(lower, gc - gc_row, NEG_INF))
            a_mat = jnp.where(strict, kk * beta * decay, 0.0)
            ve = v_all[:, sidx * dh:(sidx + 1) * dh]
            x = jnp.concatenate([ve * beta, kn * (beta * jnp.exp(gc))], axis=1)
            m = -a_mat
            x = x + _bdot(m, x)
            span = 2
            while span < c:
                m = _bdot(m, m)
                x = x + _bdot(m, x)
                span *= 2
            uw = x.astype(BF16)
            attn = jnp.where(lower, qk * decay, 0.0).astype(BF16)
            auw = jnp.dot(attn, uw, preferred_element_type=F32)
            g_last = jnp.concatenate(
                [jnp.broadcast_to(gc[(ch + 1) * c - 1:(ch + 1) * c], (c, 1)) for ch in range(nchunk)], axis=0)
            k_dec = (kn * jnp.exp(g_last - gc)).astype(BF16)
            q_eff = qn * jnp.exp(gc) - auw[:, dh:]
            o_chunks = []
            for ch in range(nchunk):
                r0 = ch * c
                bn = lax.dot_general(k_dec[r0:r0 + c], uw[r0:r0 + c], _TN, preferred_element_type=F32)
                state = state_s[sidx]
                lhs = jnp.concatenate([q_eff[r0:r0 + c], bn[:, dh:]], axis=0)
                prod = _bdot(lhs, state)
                o_chunks.append(prod[:c] + auw[r0:r0 + c, :dh])
                state_s[sidx] = state * jnp.exp(gc[r0 + c - 1:r0 + c]) - prod[c:] + bn[:, :dh]
            o = jnp.concatenate(o_chunks, axis=0)
            z = z_ref[:, sidx * dh:(sidx + 1) * dh]
            o = o * lax.rsqrt(jnp.mean(o * o, axis=-1, keepdims=True) + NORM_EPS) * on_ref[...] * (z * jax.nn.sigmoid(z))
            outs.append(o)
    o_ref[...] = jnp.concatenate(outs, axis=1).astype(o_ref.dtype)


def _dn_mix(y, bsz, seqlen, conv_w, a_log, dt_bias, o_norm):
    dh, hq_n, hv_n = DN_HEAD_DIM, DN_QK_HEADS, DN_V_HEADS
    rep = hv_n // hq_n
    rows, hq_blk = DN_ROW_BLOCK, DN_HEAD_BLOCK
    assert seqlen % rows == 0 and 2 * hv_n <= dh and hq_n % hq_blk == 0
    nb = seqlen // rows
    qk_dim, v_dim = hq_n * dh, hv_n * dh
    qb, vb = hq_blk * dh, hq_blk * rep * dh
    pad = jnp.zeros((dh - 2 * hv_n,), F32)
    alog_row = jnp.concatenate([jnp.zeros((hv_n,), F32), a_log, pad]).reshape(1, dh)
    dt_row = jnp.concatenate([jnp.zeros((hv_n,), F32), dt_bias, pad]).reshape(1, dh)
    taps = conv_w.shape[0]
    return pl.pallas_call(
        functools.partial(_dn_body, rows=rows, hq_blk=hq_blk),
        grid=(bsz, hq_n // hq_blk, nb),
        in_specs=[
            pl.BlockSpec((rows, qb), lambda b, h, i: (b * nb + i, h)),
            pl.BlockSpec((rows, qb), lambda b, h, i: (b * nb + i, qk_dim // qb + h)),
            pl.BlockSpec((rows, vb), lambda b, h, i: (b * nb + i, 2 * qk_dim // vb + h)),
            pl.BlockSpec((rows, vb), lambda b, h, i: (b * nb + i, (2 * qk_dim + v_dim) // vb + h)),
            pl.BlockSpec((rows, dh), lambda b, h, i: (b * nb + i, (2 * qk_dim + 2 * v_dim) // dh)),
            pl.BlockSpec((taps, qb), lambda b, h, i: (0, h)),
            pl.BlockSpec((taps, qb), lambda b, h, i: (0, qk_dim // qb + h)),
            pl.BlockSpec((taps, vb), lambda b, h, i: (0, 2 * qk_dim // vb + h)),
            pl.BlockSpec((1, dh), lambda b, h, i: (0, 0)),
            pl.BlockSpec((1, dh), lambda b, h, i: (0, 0)),
            pl.BlockSpec((1, dh), lambda b, h, i: (0, 0)),
        ],
        out_specs=pl.BlockSpec((rows, vb), lambda b, h, i: (b * nb + i, h)),
        out_shape=jax.ShapeDtypeStruct((bsz * seqlen, v_dim), BF16),
        scratch_shapes=[
            pltpu.VMEM((DN_CONV_HALO + rows, qb), F32), pltpu.VMEM((DN_CONV_HALO + rows, qb), F32),
            pltpu.VMEM((DN_CONV_HALO + rows, vb), F32), pltpu.VMEM((hq_blk * rep, dh, dh), F32),
            pltpu.VMEM((dh, rows), F32),
        ],
        name="dn_chunk_rule",
        compiler_params=_cparams("parallel", "parallel", "arbitrary"),
    )(y, y, y, y, y, conv_w, conv_w, conv_w, alog_row, dt_row, o_norm.reshape(1, dh))


def _gated_deltanet_res(h, u, bsz, seqlen, j, w_in, conv_w, a_log, dt_bias, o_norm, w_out):
    y = _mm(u, w_in, j, tm=2048, tn=512)
    o = _dn_mix(y, bsz, seqlen, conv_w, a_log, dt_bias, o_norm)
    return _mm(o, w_out, j, res=h, tm=1024, tn=512)


CV_ROW_BLOCK = 256
CV_CONV_HALO = 32
CV_LANE_CHUNK = 512


def _cv_body(val_ref, gate_ref, dw_ref, dwb_ref, g_ref, b_ref, o_ref, ext_s, conv_s, *, rows):
    width, ch = dw_ref.shape
    halo = CV_CONV_HALO

    @pl.when(pl.program_id(1) == 0)
    def _():
        ext_s[:halo, :] = jnp.zeros((halo, ch), F32)

    gate = gate_ref[...]
    ext_s[halo:, :] = val_ref[...] * jax.nn.sigmoid(gate)
    for c0 in range(0, ch, CV_LANE_CHUNK):
        cs = slice(c0, c0 + CV_LANE_CHUNK)
        acc = jnp.zeros((rows, CV_LANE_CHUNK), F32) + dwb_ref[:, cs]
        for k in range(width):
            off = halo - (width - 1) + k
            acc = acc + dw_ref[k:k + 1, cs] * ext_s[off:off + rows, cs]
        conv_s[:, cs] = acc
    ext_s[:halo, :] = ext_s[rows:rows + halo, :]
    u = conv_s[...]
    mu = jnp.mean(u, axis=-1, keepdims=True)
    d = u - mu
    var = jnp.mean(d * d, axis=-1, keepdims=True)
    u = d * lax.rsqrt(var + NORM_EPS) * g_ref[...] + b_ref[...]
    o_ref[...] = (u * jax.nn.sigmoid(u)).astype(o_ref.dtype)


def _cv_mix(y, bsz, seqlen, dw, dw_b, ln_g, ln_b):
    rows = CV_ROW_BLOCK
    width, ch = dw.shape
    assert seqlen % rows == 0 and width - 1 <= CV_CONV_HALO <= rows and ch % CV_LANE_CHUNK == 0
    nb = seqlen // rows
    vec = lambda v: v.reshape(1, ch)
    vspec = pl.BlockSpec((1, ch), lambda b, i: (0, 0))
    return pl.pallas_call(
        functools.partial(_cv_body, rows=rows),
        grid=(bsz, nb),
        in_specs=[
            pl.BlockSpec((rows, ch), lambda b, i: (b * nb + i, 0)),
            pl.BlockSpec((rows, ch), lambda b, i: (b * nb + i, 1)),
            pl.BlockSpec((width, ch), lambda b, i: (0, 0)),
            vspec, vspec, vspec,
        ],
        out_specs=pl.BlockSpec((rows, ch), lambda b, i: (b * nb + i, 0)),
        out_shape=jax.ShapeDtypeStruct((bsz * seqlen, ch), BF16),
        scratch_shapes=[pltpu.VMEM((CV_CONV_HALO + rows, ch), F32), pltpu.VMEM((rows, ch), F32)],
        name="conformer_conv",
        compiler_params=_cparams("parallel", "arbitrary"),
    )(y, y, dw, vec(dw_b), vec(ln_g), vec(ln_b))


def _conformer_res(h, u, bsz, seqlen, j, w_in, b_in, dw, dw_b, ln_g, ln_b, w_out, b_out):
    y = _mm(u, w_in, j, bias=b_in, tm=2048, tn=512)
    a = _cv_mix(y, bsz, seqlen, dw, dw_b, ln_g, ln_b)
    return _mm(a, w_out, j, bias=b_out, res=h, tm=2048, tn=512)


NSA_Q_TILE = 256
NSA_KEY_TILE = 1024
NSA_MASK_BIAS = 2.0 ** 60


def _nsa_compress_body(x_ref, w1_ref, w2_ref, o_ref):
    hid = jnp.dot(x_ref[...], w1_ref[...].astype(BF16), preferred_element_type=F32)
    hid = hid * jax.nn.sigmoid(hid)
    o_ref[...] = jnp.dot(hid.astype(BF16), w2_ref[...].astype(BF16), preferred_element_type=F32).astype(o_ref.dtype)


def _nsa_compress(blk, w1, w2, tr):
    two, rows, kdim = blk.shape
    hdim, dh = w2.shape[1], w2.shape[2]
    return pl.pallas_call(
        _nsa_compress_body,
        grid=(two, rows // tr),
        in_specs=[pl.BlockSpec((None, tr, kdim), lambda s, i: (s, i, 0)),
                  pl.BlockSpec((None, kdim, hdim), lambda s, i: (s, 0, 0)),
                  pl.BlockSpec((None, hdim, dh), lambda s, i: (s, 0, 0))],
        out_specs=pl.BlockSpec((None, tr, dh), lambda s, i: (s, i, 0)),
        out_shape=jax.ShapeDtypeStruct((two, rows, dh), BF16),
        name="nsa_compress",
        compiler_params=_cparams("parallel", "arbitrary"),
    )(blk, w1, w2)


def _softmax_rows(s):
    m = jnp.max(s, axis=-1, keepdims=True)
    e = jnp.exp(s - m)
    return e / jnp.sum(e, axis=-1, keepdims=True)


def _nsa_attn_body(q_ref, ks_ref, vs_ref, kw_ref, vw_ref, gl_ref, kc_ref, vc_ref, cov_ref, o_ref,
                   kaug_s, vs_s, kw_s, vw_s, m_s, l_s, acc_s, *, tq, tk, seqlen):
    qi = pl.program_id(2)
    nr = NSA_HEADS // NSA_KV_GROUPS
    dh = NSA_HEAD_DIM
    rows = nr * tq
    nsel = seqlen // NSA_SEL_BLOCK
    ncmp = seqlen // NSA_CMP_STRIDE

    @pl.when(qi == 0)
    def _():
        row_blk = lax.broadcasted_iota(jnp.int32, (seqlen, dh), 0) // NSA_SEL_BLOCK
        lane = lax.broadcasted_iota(jnp.int32, (seqlen, dh), 1)
        kaug_s[:, :dh] = ks_ref[...].astype(BF16)
        kaug_s[:, dh:] = jnp.where(lane == row_blk, 1.0, 0.0).astype(BF16)
        vs_s[...] = vs_ref[...].astype(BF16)
        kw_s[...] = kw_ref[...].astype(BF16)
        vw_s[...] = vw_ref[...].astype(BF16)

    q = q_ref[...]
    qs = jnp.concatenate([q[:, r * dh:(r + 1) * dh] for r in range(nr)], axis=0) * (dh ** -0.5)
    qb = qs.astype(BF16)
    pos = qi * tq + (lax.broadcasted_iota(jnp.int32, (rows, 1), 0) & (tq - 1))

    s = lax.dot_general(qb, kc_ref[...], _NT, preferred_element_type=F32)
    blk_end = lax.broadcasted_iota(jnp.int32, (rows, ncmp), 1) * NSA_CMP_STRIDE + (NSA_CMP_BLOCK - 1)
    valid = blk_end <= pos
    p = jnp.where(valid, _softmax_rows(jnp.where(valid, s, NEG_INF)), 0.0)
    o_cmp = jnp.dot(p.astype(BF16), vc_ref[...], preferred_element_type=F32)
    psum = p[0:tq] + p[tq:2 * tq] + p[2 * tq:3 * tq] + p[3 * tq:4 * tq]
    p_hi = psum.astype(BF16)
    p_lo = (psum - p_hi.astype(F32)).astype(BF16)
    cov = cov_ref[...]
    imp_t = (lax.dot_general(cov, p_hi, _NT, preferred_element_type=F32)
             + lax.dot_general(cov, p_lo, _NT, preferred_element_type=F32))

    jb = lax.broadcasted_iota(jnp.int32, (nsel, tq), 0)
    post = qi * tq + lax.broadcasted_iota(jnp.int32, (nsel, tq), 1)
    cur = post // NSA_SEL_BLOCK
    forced = (jb == 0) | (jb == cur) | (jb == cur - 1)
    causal = jb * NSA_SEL_BLOCK <= post
    score = jnp.where(forced, NSA_FORCE, jnp.where(causal, imp_t, -NSA_FORCE))
    rank = jnp.zeros((nsel, tq), F32)
    for k in range(nsel):
        rk = score[k:k + 1, :]
        rank = rank + jnp.where(rk > score, 1.0, jnp.where((rk == score) & (jb > k), 1.0, 0.0))
    n_top = min(NSA_TOP_N, nsel)
    bias_t = jnp.where(rank < n_top, 0.0, -NSA_MASK_BIAS)
    bias_t = jnp.concatenate([bias_t, jnp.zeros((dh - nsel, tq), F32)], axis=0)
    bias = jnp.transpose(bias_t).astype(BF16)
    qaug = jnp.concatenate([qb, jnp.concatenate([bias] * nr, axis=0)], axis=1)

    m_s[...] = jnp.full(m_s.shape, NEG_INF, F32)
    l_s[...] = jnp.zeros(l_s.shape, F32)
    acc_s[...] = jnp.zeros(acc_s.shape, F32)

    def sel_step(kt, carry, causal_mask):
        k0 = pl.multiple_of(kt * tk, tk)
        st = lax.dot_general(qaug, kaug_s[pl.ds(k0, tk), :], _NT, preferred_element_type=F32)
        if causal_mask:
            kpos = k0 + lax.broadcasted_iota(jnp.int32, (rows, tk), 1)
            st = jnp.where(kpos <= pos, st, NEG_INF)
        m_prev = m_s[...]
        m_new = jnp.maximum(m_prev, jnp.max(st, axis=-1, keepdims=True))
        alpha = jnp.exp(m_prev - m_new)
        pt = jnp.exp(st - m_new)
        l_s[...] = alpha * l_s[...] + jnp.sum(pt, axis=-1, keepdims=True)
        acc_s[...] = alpha * acc_s[...] + jnp.dot(pt.astype(BF16), vs_s[pl.ds(k0, tk), :],
                                                  preferred_element_type=F32)
        m_s[...] = m_new
        return carry

    n_full = (qi * tq) // tk
    lax.fori_loop(0, n_full, functools.partial(sel_step, causal_mask=False), 0)
    sel_step(n_full, 0, causal_mask=True)
    o_sel = acc_s[...] / l_s[...]

    wk = NSA_WINDOW + tq
    w0 = pl.multiple_of(jnp.maximum(qi * tq - NSA_WINDOW, 0), tq)
    sw = lax.dot_general(qb, kw_s[pl.ds(w0, wk), :], _NT, preferred_element_type=F32)
    diff = pos - (w0 + lax.broadcasted_iota(jnp.int32, (rows, wk), 1))
    pw = _softmax_rows(jnp.where((diff >= 0) & (diff < NSA_WINDOW), sw, NEG_INF))
    o_win = jnp.dot(pw.astype(BF16), vw_s[pl.ds(w0, wk), :], preferred_element_type=F32)

    gate = jax.nn.sigmoid(gl_ref[...])
    outs = []
    for r in range(nr):
        sl = slice(r * tq, (r + 1) * tq)
        outs.append(gate[:, 3 * r:3 * r + 1] * o_cmp[sl] + gate[:, 3 * r + 1:3 * r + 2] * o_sel[sl]
                    + gate[:, 3 * r + 2:3 * r + 3] * o_win[sl])
    o_ref[...] = jnp.concatenate(outs, axis=1).astype(o_ref.dtype)


def _nsa_cover(seqlen):
    n_sel = seqlen // NSA_SEL_BLOCK
    ncmp = seqlen // NSA_CMP_STRIDE
    per = NSA_SEL_BLOCK // NSA_CMP_STRIDE
    cover = np.zeros((n_sel, ncmp), np.float32)
    for j in range(n_sel):
        for m in range(per):
            for n in range(NSA_CMP_BLOCK // NSA_CMP_STRIDE):
                c = per * j + m - n
                if 0 <= c < ncmp - 1:
                    cover[j, c] += 1.0
    return cover


def _nsa_mix(y, bsz, seqlen, pe_k, pe_v, ck_w1, ck_w2, cv_w1, cv_w2):
    ng, nr, dh = NSA_KV_GROUPS, NSA_HEADS // NSA_KV_GROUPS, NSA_HEAD_DIM
    q_dim, kv_dim = NSA_HEADS * dh, ng * dh
    tq, tk = NSA_Q_TILE, NSA_KEY_TILE
    assert seqlen % tk == 0 and tk % tq == 0 and seqlen >= NSA_WINDOW + tq and seqlen // NSA_SEL_BLOCK <= dh
    ns = seqlen // NSA_CMP_STRIDE
    y3 = y.reshape(bsz, seqlen, -1)

    def blocks(col0, pe):
        t = y3[:, :, col0:col0 + kv_dim].reshape(bsz, ns, NSA_CMP_STRIDE, ng, dh)
        blk = jnp.concatenate([t[:, 0:ns - 1], t[:, 1:ns]], axis=2) + pe[None, None, :, None, :]
        blk = jnp.moveaxis(blk, 3, 1).reshape(bsz, ng, ns - 1, NSA_CMP_BLOCK * dh)
        blk = jnp.pad(blk, ((0, 0), (0, 0), (0, 1), (0, 0)))
        return blk.reshape(bsz * ng * ns, NSA_CMP_BLOCK * dh).astype(BF16)

    blk = jnp.stack([blocks(q_dim, pe_k), blocks(q_dim + kv_dim, pe_v)])
    cmp = _nsa_compress(blk, jnp.stack([ck_w1, cv_w1]), jnp.stack([ck_w2, cv_w2]), ns)
    cmp = cmp.reshape(2, bsz * ng, ns, dh)
    gl = y3[:, :, q_dim + 6 * kv_dim:].reshape(bsz * seqlen, ng, nr * 3)
    gl = jnp.pad(jnp.moveaxis(gl, 1, 0), ((0, 0), (0, 0), (0, dh - nr * 3)))
    cover = jnp.asarray(_nsa_cover(seqlen), BF16)
    nq = seqlen // tq
    kvc = kv_dim // dh
    c0 = q_dim // dh

    def kv_spec(section):
        return pl.BlockSpec((seqlen, dh), lambda b, g, i, s=section: (b, c0 + s * kvc + g))

    return pl.pallas_call(
        functools.partial(_nsa_attn_body, tq=tq, tk=tk, seqlen=seqlen),
        grid=(bsz, ng, nq),
        in_specs=[
            pl.BlockSpec((tq, nr * dh), lambda b, g, i: (b * nq + i, g)),
            kv_spec(2), kv_spec(3), kv_spec(4), kv_spec(5),
            pl.BlockSpec((None, tq, dh), lambda b, g, i: (g, b * nq + i, 0)),
            pl.BlockSpec((None, None, ns, dh), lambda b, g, i: (0, b * ng + g, 0, 0)),
            pl.BlockSpec((None, None, ns, dh), lambda b, g, i: (1, b * ng + g, 0, 0)),
            pl.BlockSpec((seqlen // NSA_SEL_BLOCK, ns), lambda b, g, i: (0, 0)),
        ],
        out_specs=pl.BlockSpec((tq, nr * dh), lambda b, g, i: (b * nq + i, g)),
        out_shape=jax.ShapeDtypeStruct((bsz * seqlen, q_dim), BF16),
        scratch_shapes=[
            pltpu.VMEM((seqlen, 2 * dh), BF16), pltpu.VMEM((seqlen, dh), BF16),
            pltpu.VMEM((seqlen, dh), BF16), pltpu.VMEM((seqlen, dh), BF16),
            pltpu.VMEM((nr * tq, 1), F32), pltpu.VMEM((nr * tq, 1), F32), pltpu.VMEM((nr * tq, dh), F32),
        ],
        name="nsa_attention",
        compiler_params=_cparams("parallel", "parallel", "arbitrary"),
    )(y, y, y, y, y, gl, cmp, cmp, cover)


def _nsa_res(h, u, bsz, seqlen, j, w_in, pe_k, pe_v, ck_w1, ck_w2, cv_w1, cv_w2, w_out):
    y = _mm(u, w_in, j, tm=2048, tn=256)
    o = _nsa_mix(y, bsz, seqlen, pe_k, pe_v, ck_w1, ck_w2, cv_w1, cv_w2)
    return _mm(o, w_out, j, res=h, tm=2048, tn=512)


def kernel(x, mix_norm, ffn_norm, final_norm, dn_w_in, dn_conv, dn_a_log, dn_dt_bias, dn_o_norm, dn_w_out, cv_w_in, cv_b_in, cv_dw, cv_dw_b, cv_ln_g, cv_ln_b, cv_w_out, cv_b_out, nsa_w_in, nsa_pe_k, nsa_pe_v, nsa_ck_w1, nsa_ck_w2, nsa_cv_w1, nsa_cv_w2, nsa_w_out, ffn_w1, ffn_w3, ffn_w2, moe_router, moe_w1, moe_w3, moe_w2):
    bsz, seqlen, d = x.shape
    depth = mix_norm.shape[0]
    h = x.reshape(bsz * seqlen, d)
    counts = [0, 0, 0]
    for i in range(depth):
        u = _rms_norm(h, mix_norm[i])
        kind = i % 3
        j = counts[kind]
        counts[kind] += 1
        if kind == 0:
            h = _gated_deltanet_res(h, u, bsz, seqlen, j, dn_w_in, dn_conv[j], dn_a_log[j], dn_dt_bias[j],
                                    dn_o_norm[j], dn_w_out)
        elif kind == 1:
            h = _conformer_res(h, u, bsz, seqlen, j, cv_w_in, cv_b_in[j], cv_dw[j], cv_dw_b[j], cv_ln_g[j],
                               cv_ln_b[j], cv_w_out, cv_b_out[j])
        else:
            h = _nsa_res(h, u, bsz, seqlen, j, nsa_w_in, nsa_pe_k[j], nsa_pe_v[j], nsa_ck_w1[j], nsa_ck_w2[j],
                         nsa_cv_w1[j], nsa_cv_w2[j], nsa_w_out)
        u = _rms_norm(h, ffn_norm[i])
        if i % 2 == 0:
            h = _swiglu_res(h, u, ffn_w1, ffn_w3, ffn_w2, i // 2)
        else:
            h = _moe_res(h, u, moe_router[i // 2], moe_w1, moe_w3, moe_w2, i // 2)
    return _rms_norm(h, final_norm).reshape(bsz, seqlen, d)
```
